```python
import math
import jax, jax.numpy as jnp
from jax import lax
import numpy as np

D_MODEL = 1024
BATCH = 32
SEQ = 2048
DEPTH = 1
DEC_BATCH = 4
DEC_SEQ = 8192
PAST_LEN = 128

MIX_WIDTH = D_MODEL
ATTN_WIDTH = MIX_WIDTH // 2
FOURIER_WIDTH = MIX_WIDTH - ATTN_WIDTH
N_ATTN_HEADS = 4
ATTN_HEAD_V = ATTN_WIDTH // N_ATTN_HEADS
ATTN_HEAD_QK = ATTN_HEAD_V // 2
N_FOURIER_GROUPS = 4
FOURIER_GROUP = FOURIER_WIDTH // N_FOURIER_GROUPS
IN_PROJ_WIDTH = 3 * ATTN_WIDTH + FOURIER_WIDTH
Q_BLOCK = 128
PEER_HEADS = 8
PEER_N_KEYS = 128
PEER_N_EXPERTS = PEER_N_KEYS * PEER_N_KEYS
PEER_QUERY_DIM = 128
PEER_HALF = PEER_QUERY_DIM // 2
PEER_TOPK = 16
TOKEN_BLOCK = 128
LN_EPS = 1e-5
RMS_EPS = 1e-5
DEEPNORM_ALPHA = (2 * DEPTH) ** 0.25
DEEPNORM_BETA = (8 * DEPTH) ** -0.25

kernel_name = 'hybrid_diffattn_fnet_peer_encoder'


def layer_norm(x, g, b):
    xf = x.astype(jnp.float32)
    mu = jnp.mean(xf, axis=-1, keepdims=True)
    xc = xf - mu
    var = jnp.mean(xc * xc, axis=-1, keepdims=True)
    y = xc * lax.rsqrt(var + LN_EPS) * g.astype(jnp.float32) + b.astype(jnp.float32)
    return y.astype(x.dtype)


def alibi_slopes():
    h = jnp.arange(1, N_ATTN_HEADS + 1, dtype=jnp.float32)
    return jnp.exp2(-8.0 * h / N_ATTN_HEADS)


def lambda_init_fn(layer):
    return 0.8 - 0.6 * math.exp(-0.3 * layer)


def diff_attention(q, k, v, lam, slopes):
    B, S = q.shape[0], q.shape[1]
    nb = S // Q_BLOCK
    scale = ATTN_HEAD_QK ** -0.5
    qf = (q.astype(jnp.float32) * scale).reshape(B, nb, Q_BLOCK, N_ATTN_HEADS, 2, ATTN_HEAD_QK)
    qf = qf.transpose(1, 0, 2, 3, 4, 5)
    kf = k.astype(jnp.float32)
    kpos = jnp.arange(S, dtype=jnp.float32)
    starts = jnp.arange(nb, dtype=jnp.int32) * Q_BLOCK

    def block(args):
        qb, start = args
        s = jnp.einsum('bqhcd,bkhcd->bhcqk', qb, kf)
        qpos = (start + jnp.arange(Q_BLOCK, dtype=jnp.int32)).astype(jnp.float32)
        dist = jnp.abs(qpos[:, None] - kpos[None, :])
        s = s - (slopes[:, None, None] * dist)[None, :, None]
        p = jax.nn.softmax(s, axis=-1)
        a = p[:, :, 0] - lam * p[:, :, 1]
        return jnp.einsum('bhqk,bkhe->bqhe', a.astype(v.dtype), v)

    o = lax.map(block, (qf, starts))
    return o.transpose(1, 0, 2, 3, 4).reshape(B, S, N_ATTN_HEADS, ATTN_HEAD_V)


def head_rms_norm(o, g, out_scale):
    of = o.astype(jnp.float32)
    y = of * lax.rsqrt(jnp.mean(of * of, axis=-1, keepdims=True) + RMS_EPS) * g.astype(jnp.float32)
    return (y * out_scale).astype(o.dtype)


def fourier_mix(xf, w_f, b_f):
    z = jnp.fft.fft2(xf.astype(jnp.float32), axes=(1, 3), norm='ortho').real.astype(xf.dtype)
    return jnp.einsum('bsgc,gce->bsge', z, w_f) + b_f


def peer(x, wq, subkeys, u, v):
    B, S, D = x.shape
    xt = x.reshape(-1, TOKEN_BLOCK, D)

    def chunk(xc):
        q = (xc @ wq).reshape(TOKEN_BLOCK, PEER_HEADS, 2, PEER_HALF)
        sc = jnp.einsum('thcd,hcnd->thcn', q.astype(jnp.float32), subkeys.astype(jnp.float32))
        s_top, i_top = lax.top_k(sc, PEER_TOPK)
        cand = s_top[:, :, 0, :, None] + s_top[:, :, 1, None, :]
        cidx = i_top[:, :, 0, :, None] * PEER_N_KEYS + i_top[:, :, 1, None, :]
        cand = cand.reshape(TOKEN_BLOCK, PEER_HEADS, PEER_TOPK * PEER_TOPK)
        cidx = cidx.reshape(TOKEN_BLOCK, PEER_HEADS, PEER_TOPK * PEER_TOPK)
        best, pos = lax.top_k(cand, PEER_TOPK)
        eidx = jnp.take_along_axis(cidx, pos, axis=-1)
        g = jax.nn.softmax(best, axis=-1)
        act = jax.nn.gelu(jnp.einsum('td,thkd->thk', xc, u[eidx]).astype(jnp.float32))
        coef = (g * act).astype(xc.dtype)
        return jnp.einsum('thk,thkd->td', coef, v[eidx])

    return lax.map(chunk, xt).reshape(B, S, D)


def encoder_layer(x, layer, w_in, lq1, lk1, lq2, lk2, subln_g, w_fourier, b_fourier, w_out,
                  ln1_g, ln1_b, peer_wq, peer_subkeys, peer_u, peer_v, ln2_g, ln2_b):
    B, S, _ = x.shape
    h = x @ w_in
    q = h[..., :ATTN_WIDTH].reshape(B, S, N_ATTN_HEADS, 2, ATTN_HEAD_QK)
    k = h[..., ATTN_WIDTH:2 * ATTN_WIDTH].reshape(B, S, N_ATTN_HEADS, 2, ATTN_HEAD_QK)
    v = h[..., 2 * ATTN_WIDTH:3 * ATTN_WIDTH].reshape(B, S, N_ATTN_HEADS, ATTN_HEAD_V)
    xf = h[..., 3 * ATTN_WIDTH:].reshape(B, S, N_FOURIER_GROUPS, FOURIER_GROUP)
    linit = lambda_init_fn(layer)
    lam = (jnp.exp(jnp.sum(lq1.astype(jnp.float32) * lk1.astype(jnp.float32)))
           - jnp.exp(jnp.sum(lq2.astype(jnp.float32) * lk2.astype(jnp.float32))) + linit)
    o = diff_attention(q, k, v, lam, alibi_slopes())
    o = head_rms_norm(o, subln_g, 1.0 - linit).reshape(B, S, ATTN_WIDTH)
    f = fourier_mix(xf, w_fourier, b_fourier).reshape(B, S, FOURIER_WIDTH)
    mix = jnp.concatenate([o, f], axis=-1) @ w_out
    x = layer_norm(DEEPNORM_ALPHA * x + mix, ln1_g, ln1_b)
    x = layer_norm(DEEPNORM_ALPHA * x + peer(x, peer_wq, peer_subkeys, peer_u, peer_v), ln2_g, ln2_b)
    return x


def run_trunk(x, ln_in_g, ln_in_b, w_in, lambda_q1, lambda_k1, lambda_q2, lambda_k2, subln_g,
              w_fourier, b_fourier, w_out, ln1_g, ln1_b, peer_wq, peer_subkeys, peer_u, peer_v,
              ln2_g, ln2_b):
    x = layer_norm(x, ln_in_g, ln_in_b)
    for l in range(DEPTH):
        x = encoder_layer(x, l, w_in[l], lambda_q1[l], lambda_k1[l], lambda_q2[l], lambda_k2[l],
                          subln_g[l], w_fourier[l], b_fourier[l], w_out[l], ln1_g[l], ln1_b[l],
                          peer_wq[l], peer_subkeys[l], peer_u[l], peer_v[l], ln2_g[l], ln2_b[l])
    return x


def setup_inputs(seed: int = 0) -> dict:
    key = jax.random.key(seed)
    ks = jax.random.split(key, 24)
    f32 = jnp.float32

    def nrm(k, shape, scale):
        return jax.random.normal(k, shape, f32) * scale

    col_scale = jnp.concatenate([jnp.ones((2 * ATTN_WIDTH,), f32),
                                 jnp.full((ATTN_WIDTH + FOURIER_WIDTH,), DEEPNORM_BETA, f32)])
    return {
        'x_prompt': nrm(ks[0], (BATCH, SEQ, D_MODEL), 1.0),
        'x_sample': nrm(ks[1], (DEC_BATCH, DEC_SEQ, D_MODEL), 1.0),
        'ln_in_g': 1.0 + nrm(ks[2], (D_MODEL,), 0.02),
        'ln_in_b': nrm(ks[3], (D_MODEL,), 0.02),
        'w_in': nrm(ks[4], (DEPTH, D_MODEL, IN_PROJ_WIDTH), D_MODEL ** -0.5) * col_scale,
        'lambda_q1': nrm(ks[5], (DEPTH, ATTN_HEAD_QK), 0.1),
        'lambda_k1': nrm(ks[6], (DEPTH, ATTN_HEAD_QK), 0.1),
        'lambda_q2': nrm(ks[7], (DEPTH, ATTN_HEAD_QK), 0.1),
        'lambda_k2': nrm(ks[8], (DEPTH, ATTN_HEAD_QK), 0.1),
        'subln_g': 1.0 + nrm(ks[9], (DEPTH, ATTN_HEAD_V), 0.02),
        'w_fourier': nrm(ks[10], (DEPTH, N_FOURIER_GROUPS, FOURIER_GROUP, FOURIER_GROUP),
                         FOURIER_GROUP ** -0.5 * DEEPNORM_BETA),
        'b_fourier': nrm(ks[11], (DEPTH, N_FOURIER_GROUPS, FOURIER_GROUP), 0.02),
        'w_out': nrm(ks[12], (DEPTH, MIX_WIDTH, D_MODEL), MIX_WIDTH ** -0.5 * DEEPNORM_BETA),
        'ln1_g': 1.0 + nrm(ks[13], (DEPTH, D_MODEL), 0.02),
        'ln1_b': nrm(ks[14], (DEPTH, D_MODEL), 0.02),
        'peer_wq': nrm(ks[15], (DEPTH, D_MODEL, PEER_HEADS * PEER_QUERY_DIM), D_MODEL ** -0.5),
        'peer_subkeys': nrm(ks[16], (DEPTH, PEER_HEADS, 2, PEER_N_KEYS, PEER_HALF), PEER_HALF ** -0.5),
        'peer_u': nrm(ks[17], (DEPTH, PEER_N_EXPERTS, D_MODEL), D_MODEL ** -0.5),
        'peer_v': nrm(ks[18], (DEPTH, PEER_N_EXPERTS, D_MODEL), DEEPNORM_BETA),
        'ln2_g': 1.0 + nrm(ks[19], (DEPTH, D_MODEL), 0.02),
        'ln2_b': nrm(ks[20], (DEPTH, D_MODEL), 0.02),
    }


def reference(x_prompt, x_sample, ln_in_g, ln_in_b, w_in, lambda_q1, lambda_k1, lambda_q2,
              lambda_k2, subln_g, w_fourier, b_fourier, w_out, ln1_g, ln1_b, peer_wq,
              peer_subkeys, peer_u, peer_v, ln2_g, ln2_b):
    y_prompt = run_trunk(x_prompt, ln_in_g, ln_in_b, w_in, lambda_q1, lambda_k1, lambda_q2,
                         lambda_k2, subln_g, w_fourier, b_fourier, w_out, ln1_g, ln1_b, peer_wq,
                         peer_subkeys, peer_u, peer_v, ln2_g, ln2_b)
    y_sample = run_trunk(x_sample, ln_in_g, ln_in_b, w_in, lambda_q1, lambda_k1, lambda_q2,
                         lambda_k2, subln_g, w_fourier, b_fourier, w_out, ln1_g, ln1_b, peer_wq,
                         peer_subkeys, peer_u, peer_v, ln2_g, ln2_b)
    return (y_prompt, y_sample)
```

```python
import functools
import math

import jax
import jax.numpy as jnp
from jax import lax
from jax.experimental import pallas as pl
from jax.experimental.pallas import tpu as pltpu

F32 = jnp.float32
BF16 = jnp.bfloat16

LN_EPS = 1e-5
RMS_EPS = 1e-5
N_ATTN_HEADS = 4
ATTN_HEAD_V = 128
ATTN_HEAD_QK = 64
N_FOURIER_GROUPS = 4
FOURIER_GROUP = 128
PEER_HEADS = 8
PEER_N_KEYS = 128
PEER_HALF = 64
PEER_TOPK = 16
PICKS = PEER_HEADS * PEER_TOPK

LANES = 128
SUBLANES = 8
VMEM_LIMIT = 56 * 1024 * 1024

NT_DIMS = (((1,), (1,)), ((), ()))


def _params(sem, vmem=VMEM_LIMIT):
    return pltpu.CompilerParams(dimension_semantics=sem, vmem_limit_bytes=vmem)


def _layer_norm(x, g, b):
    mu = jnp.mean(x, axis=-1, keepdims=True)
    xc = x - mu
    var = jnp.mean(xc * xc, axis=-1, keepdims=True)
    return xc * lax.rsqrt(var + LN_EPS) * g + b


def _prep_kernel(win_f_ref, wf_ref, c_ref, s_ref, wab_ref):
    hp = lax.Precision.HIGHEST
    for g in range(N_FOURIER_GROUPS):
        wg = wf_ref[g]
        cw = jnp.dot(c_ref[...], wg, precision=hp, preferred_element_type=F32)
        sw = jnp.dot(s_ref[...], wg, precision=hp, preferred_element_type=F32)
        wi = win_f_ref[:, g * FOURIER_GROUP:(g + 1) * FOURIER_GROUP]
        a = jnp.dot(wi, cw, precision=hp, preferred_element_type=F32)
        b = jnp.dot(wi, sw, precision=hp, preferred_element_type=F32)
        wab_ref[:, g * FOURIER_GROUP:(g + 1) * FOURIER_GROUP] = a.astype(BF16)
        wab_ref[:, (N_FOURIER_GROUPS + g) * FOURIER_GROUP:
                (N_FOURIER_GROUPS + g + 1) * FOURIER_GROUP] = b.astype(BF16)


def _prep(win_f, w_fourier):
    d = win_f.shape[0]
    n = FOURIER_GROUP
    idx = jnp.arange(n, dtype=jnp.int32)
    ang = ((idx[:, None] * idx[None, :]) % n).astype(F32) * (2.0 * math.pi / n)
    cmat = jnp.cos(ang) * (n ** -0.5)
    smat = jnp.sin(ang) * (n ** -0.5)
    return pl.pallas_call(
        _prep_kernel,
        out_shape=jax.ShapeDtypeStruct((d, 2 * N_FOURIER_GROUPS * n), BF16),
        name="prep",
    )(win_f, w_fourier, cmat, smat)


def _inproj_kernel(x_ref, g_ref, b_ref, wqkv_ref, wab_ref, q_ref, k_ref, v_ref, a_ref, bb_ref):
    xn = _layer_norm(x_ref[...], g_ref[...], b_ref[...]).astype(BF16)
    h = jnp.dot(xn, wqkv_ref[...], preferred_element_type=F32)
    aw = q_ref.shape[1]
    q_ref[...] = (h[:, :aw] * (ATTN_HEAD_QK ** -0.5)).astype(BF16)
    k_ref[...] = h[:, aw:2 * aw].astype(BF16)
    v_ref[...] = h[:, 2 * aw:3 * aw].astype(BF16)
    ab = jnp.dot(xn, wab_ref[...], preferred_element_type=F32)
    fw = a_ref.shape[1]
    a_ref[...] = ab[:, :fw].astype(BF16)
    bb_ref[...] = ab[:, fw:].astype(BF16)


def _in_proj(x2, g, b, wqkv, wab, tm=512):
    t, d = x2.shape
    aw = wqkv.shape[1] // 3
    fw = wab.shape[1] // 2
    row = lambda i: (i, 0)
    full = lambda i: (0, 0)
    outs = [jax.ShapeDtypeStruct((t, aw), BF16)] * 3 + [jax.ShapeDtypeStruct((t, fw), BF16)] * 2
    return pl.pallas_call(
        _inproj_kernel,
        grid=(t // tm,),
        in_specs=[pl.BlockSpec((tm, d), row), pl.BlockSpec((1, d), full), pl.BlockSpec((1, d), full),
                  pl.BlockSpec(wqkv.shape, full), pl.BlockSpec(wab.shape, full)],
        out_specs=[pl.BlockSpec((tm, aw), row)] * 3 + [pl.BlockSpec((tm, fw), row)] * 2,
        out_shape=outs,
        compiler_params=_params(("parallel",)),
        name="in_proj",
    )(x2, g, b, wqkv, wab)


def _attn_kernel(lamv_ref, slope_ref, q_ref, k_ref, v_ref, g_ref, o_ref,
                 m_ref, l_ref, acc_ref, *, tq, tk, linit):
    h = pl.program_id(1)
    i = pl.program_id(2)
    s_len = k_ref.shape[0]
    slope = slope_ref[h]
    lv = lamv_ref[...]
    lam = (jnp.exp(jnp.sum(lv[0:1] * lv[1:2], axis=-1, keepdims=True))
           - jnp.exp(jnp.sum(lv[2:3] * lv[3:4], axis=-1, keepdims=True)) + linit)

    q = q_ref[...]
    lane = lax.broadcasted_iota(jnp.int32, q.shape, 1)
    zero = jnp.zeros_like(q)
    qh = (jnp.where(lane < ATTN_HEAD_QK, q, zero), jnp.where(lane >= ATTN_HEAD_QK, q, zero))
    qpos = (i * tq + lax.broadcasted_iota(jnp.int32, (tq, 1), 0)).astype(F32)

    m_ref[...] = jnp.full(m_ref.shape, -jnp.inf, F32)
    l_ref[...] = jnp.zeros(l_ref.shape, F32)
    acc_ref[...] = jnp.zeros(acc_ref.shape, F32)

    def body(j, carry):
        start = pl.multiple_of(j * tk, tk)
        kb = k_ref[pl.ds(start, tk), :]
        vb = v_ref[pl.ds(start, tk), :]
        kpos = (j * tk + lax.broadcasted_iota(jnp.int32, (1, tk), 1)).astype(F32)
        bias = slope * jnp.abs(qpos - kpos)
        for c in range(2):
            s = lax.dot_general(qh[c], kb, NT_DIMS, preferred_element_type=F32) - bias
            m_old = m_ref[c]
            m_new = jnp.maximum(m_old, jnp.max(s, axis=-1, keepdims=True))
            p = jnp.exp(s - m_new)
            alpha = jnp.exp(m_old - m_new)
            l_ref[c] = alpha * l_ref[c] + jnp.sum(p, axis=-1, keepdims=True)
            acc_ref[c] = alpha * acc_ref[c] + jnp.dot(p.astype(BF16), vb, preferred_element_type=F32)
            m_ref[c] = m_new
        return carry

    lax.fori_loop(0, s_len // tk, body, 0)
    o = acc_ref[0] / l_ref[0] - lam * (acc_ref[1] / l_ref[1])
    y = o * lax.rsqrt(jnp.mean(o * o, axis=-1, keepdims=True) + RMS_EPS) * g_ref[...]
    o_ref[...] = (y * (1.0 - linit)).astype(o_ref.dtype)


def _attention(q, k, v, lamv, slopes, subln_g, linit, tq=256, tk=512):
    b, s, aw = q.shape
    hv = ATTN_HEAD_V
    qspec = pl.BlockSpec((None, tq, hv), lambda bi, hi, qi: (bi, qi, hi))
    kvspec = pl.BlockSpec((None, s, hv), lambda bi, hi, qi: (bi, 0, hi))
    return pl.pallas_call(
        functools.partial(_attn_kernel, tq=tq, tk=tk, linit=linit),
        grid=(b, aw // hv, s // tq),
        in_specs=[pl.BlockSpec((4, ATTN_HEAD_QK), lambda bi, hi, qi: (0, 0)),
                  pl.BlockSpec(memory_space=pltpu.SMEM),
                  qspec, kvspec, kvspec,
                  pl.BlockSpec((1, hv), lambda bi, hi, qi: (0, 0))],
        out_specs=qspec,
        out_shape=jax.ShapeDtypeStruct((b, s, aw), BF16),
        scratch_shapes=[pltpu.VMEM((2, tq, 1), F32), pltpu.VMEM((2, tq, 1), F32),
                        pltpu.VMEM((2, tq, hv), F32)],
        compiler_params=_params(("parallel", "parallel", "arbitrary")),
        name="attn",
    )(lamv, slopes, q, k, v, subln_g)


def _fourier_kernel(c_ref, ns_ref, a_ref, b_ref, bias_ref, f_ref, acc_ref):
    kk = pl.program_id(2)

    @pl.when(kk == 0)
    def _():
        acc_ref[...] = jnp.zeros(acc_ref.shape, F32)

    acc_ref[...] += (jnp.dot(c_ref[...], a_ref[...], preferred_element_type=F32)
                     + jnp.dot(ns_ref[...], b_ref[...], preferred_element_type=F32))

    @pl.when(kk == pl.num_programs(2) - 1)
    def _():
        f_ref[...] = (acc_ref[...] + bias_ref[...]).astype(f_ref.dtype)


def _dft_matrices(s):
    idx = jnp.arange(s, dtype=jnp.int32)
    ang = ((idx[:, None] * idx[None, :]) % s).astype(F32) * (2.0 * math.pi / s)
    scale = s ** -0.5
    return (jnp.cos(ang) * scale).astype(BF16), (jnp.sin(ang) * (-scale)).astype(BF16)


def _fourier(a, bm, bias, tm=1024, tk=1024):
    b, s, fw = a.shape
    cmat, nsmat = _dft_matrices(s)
    mspec = pl.BlockSpec((tm, tk), lambda bi, i, kk: (i, kk))
    xspec = pl.BlockSpec((None, tk, fw), lambda bi, i, kk: (bi, kk, 0))
    return pl.pallas_call(
        _fourier_kernel,
        grid=(b, s // tm, s // tk),
        in_specs=[mspec, mspec, xspec, xspec, pl.BlockSpec((1, fw), lambda bi, i, kk: (0, 0))],
        out_specs=pl.BlockSpec((None, tm, fw), lambda bi, i, kk: (bi, i, 0)),
        out_shape=jax.ShapeDtypeStruct((b, s, fw), BF16),
        scratch_shapes=[pltpu.VMEM((tm, fw), F32)],
        compiler_params=_params(("parallel", "parallel", "arbitrary")),
        name="fourier",
    )(cmat, nsmat, a, bm, bias)


def _outproj_kernel(o_ref, f_ref, x_ref, gi_ref, bi_ref, wo_ref, g1_ref, b1_ref, wq_ref,
                    x1_ref, qp_ref, *, alpha):
    aw = o_ref.shape[1]
    mix = (jnp.dot(o_ref[...], wo_ref[:aw, :], preferred_element_type=F32)
           + jnp.dot(f_ref[...], wo_ref[aw:, :], preferred_element_type=F32))
    xn = _layer_norm(x_ref[...], gi_ref[...], bi_ref[...])
    x1 = _layer_norm(alpha * xn + mix, g1_ref[...], b1_ref[...])
    x1_ref[...] = x1
    qp_ref[...] = jnp.dot(x1.astype(BF16), wq_ref[...], preferred_element_type=F32)


def _out_proj(o2, f2, x2, gi, bi, wo, g1, b1, wq, alpha, tm=512):
    t, d = x2.shape
    row = lambda i: (i, 0)
    full = lambda i: (0, 0)
    vec = pl.BlockSpec((1, d), full)
    return pl.pallas_call(
        functools.partial(_outproj_kernel, alpha=alpha),
        grid=(t // tm,),
        in_specs=[pl.BlockSpec((tm, o2.shape[1]), row), pl.BlockSpec((tm, f2.shape[1]), row),
                  pl.BlockSpec((tm, d), row), vec, vec, pl.BlockSpec(wo.shape, full), vec, vec,
                  pl.BlockSpec(wq.shape, full)],
        out_specs=[pl.BlockSpec((tm, d), row), pl.BlockSpec((tm, wq.shape[1]), row)],
        out_shape=[jax.ShapeDtypeStruct((t, d), F32), jax.ShapeDtypeStruct((t, wq.shape[1]), F32)],
        compiler_params=_params(("parallel",)),
        name="out_proj",
    )(o2, f2, x2, gi, bi, wo, g1, b1, wq)


def _topk_cols(s, payload, k):
    n = s.shape[0]
    ridx = lax.broadcasted_iota(jnp.int32, s.shape, 0)
    vals, pays = [], []
    for _ in range(k):
        m = jnp.max(s, axis=0, keepdims=True)
        am = jnp.min(jnp.where(s == m, ridx, n), axis=0, keepdims=True)
        hit = ridx == am
        vals.append(m)
        pays.append(jnp.max(jnp.where(hit, payload, -1), axis=0, keepdims=True))
        s = jnp.where(hit, -jnp.inf, s)
    return jnp.concatenate(vals, axis=0), jnp.concatenate(pays, axis=0)


def _topk_kernel(q_ref, sk_ref, e_ref, g_ref):
    q = q_ref[...].astype(BF16)
    tb = q.shape[0]
    key_idx = lax.broadcasted_iota(jnp.int32, (PEER_N_KEYS, tb), 0)
    tops = []
    for c in range(2):
        sc = lax.dot_general(sk_ref[c], q, NT_DIMS, preferred_element_type=F32)
        tops.append(_topk_cols(sc, key_idx, PEER_TOPK))
    (s1, i1), (s2, i2) = tops
    kk = PEER_TOPK
    cand = (s1[:, None, :] + s2[None, :, :]).reshape(kk * kk, tb)
    cidx = (i1[:, None, :] * PEER_N_KEYS + i2[None, :, :]).reshape(kk * kk, tb)
    best, eidx = _topk_cols(cand, cidx, kk)
    ex = jnp.exp(best - best[0:1])
    e_ref[...] = eidx
    g_ref[...] = ex / jnp.sum(ex, axis=0, keepdims=True)


def _topk(qp, sk_ext, tb=256):
    t = qp.shape[0]
    hp = PEER_HEADS
    ospec = pl.BlockSpec((None, PEER_TOPK, tb), lambda i, h: (h, 0, i))
    return pl.pallas_call(
        _topk_kernel,
        grid=(t // tb, hp),
        in_specs=[pl.BlockSpec((tb, LANES), lambda i, h: (i, h)),
                  pl.BlockSpec((None, 2, PEER_N_KEYS, LANES), lambda i, h: (h, 0, 0, 0))],
        out_specs=[ospec, ospec],
        out_shape=[jax.ShapeDtypeStruct((hp, PEER_TOPK, t), jnp.int32),
                   jax.ShapeDtypeStruct((hp, PEER_TOPK, t), F32)],
        compiler_params=_params(("parallel", "parallel")),
        name="topk",
    )(qp, sk_ext)


HALF_ROWS = 4


def _pack_table(w):
    n, d = w.shape
    bits = lax.bitcast_convert_type(w.astype(BF16), jnp.uint16).astype(jnp.uint32)
    lo = bits[:, :d // 2].reshape(n, HALF_ROWS, LANES)
    hi = bits[:, d // 2:].reshape(n, HALF_ROWS, LANES)
    return lo | (hi << 16)


def _gather_pairs(tab_ref, eidx_ref, base, n_pairs, start_pick):
    out = []
    for p in range(n_pairs):
        kpick = start_pick + 2 * p
        wa = tab_ref[eidx_ref[base + kpick]]
        wb = tab_ref[eidx_ref[base + kpick + 1]]
        out.append(pltpu.bitcast(jnp.concatenate([wa, wb], axis=0), BF16))
    return out


TOK_UNROLL = 8


def _peer_u_kernel(eidx_ref, x_ref, g_ref, tab_ref, coef_ref, xbuf, abuf):
    tb = x_ref.shape[0]
    nblk = x_ref.shape[1] // LANES
    for j in range(nblk):
        h, s = divmod(j, HALF_ROWS)
        xbuf[pl.ds(2 * s + h, tb, stride=nblk), :] = x_ref[:, j * LANES:(j + 1) * LANES]

    chunk = 32
    sel = (lax.broadcasted_iota(jnp.int32, (chunk, chunk * 8), 1) // 8
           == lax.broadcasted_iota(jnp.int32, (chunk, chunk * 8), 0)).astype(BF16)
    ones = jnp.ones((SUBLANES, LANES), BF16)
    rowid = lax.broadcasted_iota(jnp.int32, (SUBLANES, LANES), 0)

    def body(grp, carry):
        acts = jnp.zeros((SUBLANES, LANES), F32)
        for tt in range(TOK_UNROLL):
            t = grp * TOK_UNROLL + tt
            base = t * PICKS
            x8 = xbuf[pl.ds(pl.multiple_of(t * nblk, nblk), nblk), :].astype(BF16)
            x16 = jnp.concatenate([x8, x8], axis=0)
            rows = []
            for c in range(PICKS // chunk):
                tiles = _gather_pairs(tab_ref, eidx_ref, base, chunk // 2, c * chunk)
                prod = jnp.concatenate([w * x16 for w in tiles], axis=0)
                rows.append(jnp.dot(sel, prod, preferred_element_type=F32))
            a = jnp.concatenate(rows, axis=0)
            a_hi = a.astype(BF16)
            a_lo = (a - a_hi.astype(F32)).astype(BF16)
            act = (lax.dot_general(ones, a_hi, NT_DIMS, preferred_element_type=F32)
                   + lax.dot_general(ones, a_lo, NT_DIMS, preferred_element_type=F32))
            acts = jnp.where(rowid == tt, act, acts)
        abuf[pl.ds(pl.multiple_of(grp * TOK_UNROLL, TOK_UNROLL), TOK_UNROLL), :] = acts
        return carry

    lax.fori_loop(0, tb // TOK_UNROLL, body, 0)
    coef_ref[...] = g_ref[...] * jax.nn.gelu(abuf[...])


def _peer_u(eidx_flat, x1, gates, tab, tb=256):
    t, d = x1.shape
    row = lambda i: (i, 0)
    return pl.pallas_call(
        _peer_u_kernel,
        grid=(t // tb,),
        in_specs=[pl.BlockSpec((tb * PICKS,), lambda i: (i,), memory_space=pltpu.SMEM),
                  pl.BlockSpec((tb, d), row), pl.BlockSpec((tb, PICKS), row),
                  pl.BlockSpec(memory_space=pltpu.VMEM)],
        out_specs=pl.BlockSpec((tb, PICKS), row),
        out_shape=jax.ShapeDtypeStruct((t, PICKS), F32),
        scratch_shapes=[pltpu.VMEM((tb * (d // LANES), LANES), F32), pltpu.VMEM((tb, PICKS), F32)],
        compiler_params=_params(("arbitrary",)),
        name="peer_u",
    )(eidx_flat, x1, gates, tab)


def _peer_v_kernel(eidx_ref, coef_ref, x_ref, g_ref, b_ref, tab_ref, y_ref, crep, obuf, *, alpha):
    tb, d = x_ref.shape
    nblk = d // LANES
    rep = d // PICKS
    expand = (lax.broadcasted_iota(jnp.int32, (PICKS, d), 1) // rep
              == lax.broadcasted_iota(jnp.int32, (PICKS, d), 0)).astype(BF16)
    crep[...] = jnp.dot(coef_ref[...].astype(BF16), expand, preferred_element_type=F32)
    jj = lax.broadcasted_iota(jnp.int32, (nblk, d), 1) % rep
    mask = (jj % 2) * HALF_ROWS + jj // 2 == lax.broadcasted_iota(jnp.int32, (nblk, d), 0)

    def body(grp, carry):
        cg = crep[pl.ds(pl.multiple_of(grp * TOK_UNROLL, TOK_UNROLL), TOK_UNROLL), :]
        for tt in range(TOK_UNROLL):
            t = grp * TOK_UNROLL + tt
            lhs = jnp.where(mask, jnp.broadcast_to(cg[tt:tt + 1, :], (nblk, d)), 0.0).astype(BF16)
            tiles = _gather_pairs(tab_ref, eidx_ref, t * PICKS, PICKS // 2, 0)
            rhs = jnp.concatenate(tiles, axis=0)
            obuf[pl.ds(pl.multiple_of(t * nblk, nblk), nblk), :] = jnp.dot(
                lhs, rhs, preferred_element_type=F32)
        return carry

    lax.fori_loop(0, tb // TOK_UNROLL, body, 0)
    peer = jnp.concatenate([obuf[pl.ds(j, tb, stride=nblk), :] for j in range(nblk)], axis=1)
    y_ref[...] = _layer_norm(alpha * x_ref[...] + peer, g_ref[...], b_ref[...])


def _peer_v(eidx_flat, coef, x1, g2, b2, tab, alpha, tb=256):
    t, d = x1.shape
    row = lambda i: (i, 0)
    vec = pl.BlockSpec((1, d), lambda i: (0, 0))
    return pl.pallas_call(
        functools.partial(_peer_v_kernel, alpha=alpha),
        grid=(t // tb,),
        in_specs=[pl.BlockSpec((tb * PICKS,), lambda i: (i,), memory_space=pltpu.SMEM),
                  pl.BlockSpec((tb, PICKS), row), pl.BlockSpec((tb, d), row), vec, vec,
                  pl.BlockSpec(memory_space=pltpu.VMEM)],
        out_specs=pl.BlockSpec((tb, d), row),
        out_shape=jax.ShapeDtypeStruct((t, d), F32),
        scratch_shapes=[pltpu.VMEM((tb, d), F32), pltpu.VMEM((tb * (d // LANES), LANES), F32)],
        compiler_params=_params(("arbitrary",)),
        name="peer_v",
    )(eidx_flat, coef, x1, g2, b2, tab)


def _trunk(x, p):
    b, s, d = x.shape
    t = b * s
    depth = p["w_in"].shape[0]
    alpha = (2 * depth) ** 0.25
    x2 = x.reshape(t, d)
    cur = x2
    gi, bi = p["ln_in_g"].reshape(1, d), p["ln_in_b"].reshape(1, d)
    for l in range(depth):
        assert l == 0, "single-layer trunk"
        linit = 0.8 - 0.6 * math.exp(-0.3 * l)
        aw = N_ATTN_HEADS * ATTN_HEAD_V
        w_in = p["w_in"][l]
        wab = _prep(w_in[:, 3 * aw:], p["w_fourier"][l])
        q, k, v, a, bm = _in_proj(cur, gi, bi, w_in[:, :3 * aw].astype(BF16), wab)
        lamv = jnp.stack([p["lambda_q1"][l], p["lambda_k1"][l], p["lambda_q2"][l], p["lambda_k2"][l]])
        hh = jnp.arange(1, N_ATTN_HEADS + 1, dtype=F32)
        slopes = jnp.exp2(-8.0 * hh / N_ATTN_HEADS)
        o = _attention(q.reshape(b, s, aw), k.reshape(b, s, aw), v.reshape(b, s, aw), lamv, slopes,
                       p["subln_g"][l].reshape(1, ATTN_HEAD_V), linit)
        f = _fourier(a.reshape(b, s, -1), bm.reshape(b, s, -1), p["b_fourier"][l].reshape(1, -1))
        x1, qp = _out_proj(o.reshape(t, aw), f.reshape(t, -1), cur, gi, bi,
                           p["w_out"][l].astype(BF16), p["ln1_g"][l].reshape(1, d),
                           p["ln1_b"][l].reshape(1, d), p["peer_wq"][l].astype(BF16), alpha)
        sk = p["peer_subkeys"][l]
        z = jnp.zeros_like(sk)
        sk_ext = jnp.stack([jnp.concatenate([sk[:, 0], z[:, 0]], axis=-1),
                            jnp.concatenate([z[:, 1], sk[:, 1]], axis=-1)], axis=1).astype(BF16)
        e_t, g_t = _topk(qp, sk_ext)
        eidx = e_t.reshape(PICKS, t).T.reshape(t * PICKS)
        gates = g_t.reshape(PICKS, t).T
        coef = _peer_u(eidx, x1, gates, _pack_table(p["peer_u"][l]))
        cur = _peer_v(eidx, coef, x1, p["ln2_g"][l].reshape(1, d), p["ln2_b"][l].reshape(1, d),
                      _pack_table(p["peer_v"][l]), alpha)
    return cur.reshape(b, s, d)


def kernel(x_prompt, x_sample, ln_in_g, ln_in_b, w_in, lambda_q1, lambda_k1, lambda_q2, lambda_k2,
           subln_g, w_fourier, b_fourier, w_out, ln1_g, ln1_b, peer_wq, peer_subkeys, peer_u, peer_v,
           ln2_g, ln2_b):
    p = dict(ln_in_g=ln_in_g, ln_in_b=ln_in_b, w_in=w_in, lambda_q1=lambda_q1, lambda_k1=lambda_k1,
             lambda_q2=lambda_q2, lambda_k2=lambda_k2, subln_g=subln_g, w_fourier=w_fourier,
             b_fourier=b_fourier, w_out=w_out, ln1_g=ln1_g, ln1_b=ln1_b, peer_wq=peer_wq,
             peer_subkeys=peer_subkeys, peer_u=peer_u, peer_v=peer_v, ln2_g=ln2_g, ln2_b=ln2_b)
    return (_trunk(x_prompt, p), _trunk(x_sample, p))
```

```python
import functools
import math

import jax
import jax.numpy as jnp
from jax import lax
from jax.experimental import pallas as pl
from jax.experimental.pallas import tpu as pltpu

F32 = jnp.float32
BF16 = jnp.bfloat16

LN_EPS = 1e-5
RMS_EPS = 1e-5
N_ATTN_HEADS = 4
ATTN_HEAD_V = 128
ATTN_HEAD_QK = 64
N_FOURIER_GROUPS = 4
FOURIER_GROUP = 128
PEER_HEADS = 8
PEER_N_KEYS = 128
PEER_HALF = 64
PEER_TOPK = 16
PICKS = PEER_HEADS * PEER_TOPK

LANES = 128
SUBLANES = 8
VMEM_LIMIT = 56 * 1024 * 1024

NT_DIMS = (((1,), (1,)), ((), ()))


def _params(sem, vmem=VMEM_LIMIT):
    return pltpu.CompilerParams(dimension_semantics=sem, vmem_limit_bytes=vmem)


def _layer_norm(x, g, b):
    mu = jnp.mean(x, axis=-1, keepdims=True)
    xc = x - mu
    var = jnp.mean(xc * xc, axis=-1, keepdims=True)
    return xc * lax.rsqrt(var + LN_EPS) * g + b


def _prep_kernel(win_f_ref, wf_ref, c_ref, s_ref, wab_ref):
    hp = lax.Precision.HIGHEST
    for g in range(N_FOURIER_GROUPS):
        wg = wf_ref[g]
        cw = jnp.dot(c_ref[...], wg, precision=hp, preferred_element_type=F32)
        sw = jnp.dot(s_ref[...], wg, precision=hp, preferred_element_type=F32)
        wi = win_f_ref[:, g * FOURIER_GROUP:(g + 1) * FOURIER_GROUP]
        a = jnp.dot(wi, cw, precision=hp, preferred_element_type=F32)
        b = jnp.dot(wi, sw, precision=hp, preferred_element_type=F32)
        wab_ref[:, g * FOURIER_GROUP:(g + 1) * FOURIER_GROUP] = a.astype(BF16)
        wab_ref[:, (N_FOURIER_GROUPS + g) * FOURIER_GROUP:
                (N_FOURIER_GROUPS + g + 1) * FOURIER_GROUP] = b.astype(BF16)


def _prep(win_f, w_fourier):
    d = win_f.shape[0]
    n = FOURIER_GROUP
    idx = jnp.arange(n, dtype=jnp.int32)
    ang = ((idx[:, None] * idx[None, :]) % n).astype(F32) * (2.0 * math.pi / n)
    cmat = jnp.cos(ang) * (n ** -0.5)
    smat = jnp.sin(ang) * (n ** -0.5)
    return pl.pallas_call(
        _prep_kernel,
        out_shape=jax.ShapeDtypeStruct((d, 2 * N_FOURIER_GROUPS * n), BF16),
        name="prep",
    )(win_f, w_fourier, cmat, smat)


def _inproj_kernel(x_ref, g_ref, b_ref, wqkv_ref, wab_ref, q_ref, k_ref, v_ref, a_ref, bb_ref):
    xn = _layer_norm(x_ref[...], g_ref[...], b_ref[...]).astype(BF16)
    h = jnp.dot(xn, wqkv_ref[...], preferred_element_type=F32)
    aw = q_ref.shape[1]
    q_ref[...] = (h[:, :aw] * (ATTN_HEAD_QK ** -0.5)).astype(BF16)
    k_ref[...] = h[:, aw:2 * aw].astype(BF16)
    v_ref[...] = h[:, 2 * aw:3 * aw].astype(BF16)
    ab = jnp.dot(xn, wab_ref[...], preferred_element_type=F32)
    fw = a_ref.shape[1]
    a_ref[...] = ab[:, :fw].astype(BF16)
    bb_ref[...] = ab[:, fw:].astype(BF16)


def _in_proj(x2, g, b, wqkv, wab, tm=512):
    t, d = x2.shape
    aw = wqkv.shape[1] // 3
    fw = wab.shape[1] // 2
    row = lambda i: (i, 0)
    full = lambda i: (0, 0)
    outs = [jax.ShapeDtypeStruct((t, aw), BF16)] * 3 + [jax.ShapeDtypeStruct((t, fw), BF16)] * 2
    return pl.pallas_call(
        _inproj_kernel,
        grid=(t // tm,),
        in_specs=[pl.BlockSpec((tm, d), row), pl.BlockSpec((1, d), full), pl.BlockSpec((1, d), full),
                  pl.BlockSpec(wqkv.shape, full), pl.BlockSpec(wab.shape, full)],
        out_specs=[pl.BlockSpec((tm, aw), row)] * 3 + [pl.BlockSpec((tm, fw), row)] * 2,
        out_shape=outs,
        compiler_params=_params(("parallel",)),
        name="in_proj",
    )(x2, g, b, wqkv, wab)


def _attn_kernel(lamv_ref, slope_ref, q_ref, k_ref, v_ref, g_ref, o_ref,
                 m_ref, l_ref, acc_ref, *, tq, tk, linit):
    h = pl.program_id(1)
    i = pl.program_id(2)
    s_len = k_ref.shape[0]
    slope = slope_ref[h]
    lv = lamv_ref[...]
    lam = (jnp.exp(jnp.sum(lv[0:1] * lv[1:2], axis=-1, keepdims=True))
           - jnp.exp(jnp.sum(lv[2:3] * lv[3:4], axis=-1, keepdims=True)) + linit)

    q = q_ref[...]
    lane = lax.broadcasted_iota(jnp.int32, q.shape, 1)
    zero = jnp.zeros_like(q)
    qh = (jnp.where(lane < ATTN_HEAD_QK, q, zero), jnp.where(lane >= ATTN_HEAD_QK, q, zero))
    qpos = (i * tq + lax.broadcasted_iota(jnp.int32, (tq, 1), 0)).astype(F32)

    m_ref[...] = jnp.full(m_ref.shape, -jnp.inf, F32)
    l_ref[...] = jnp.zeros(l_ref.shape, F32)
    acc_ref[...] = jnp.zeros(acc_ref.shape, F32)

    def body(j, carry):
        start = pl.multiple_of(j * tk, tk)
        kb = k_ref[pl.ds(start, tk), :]
        vb = v_ref[pl.ds(start, tk), :]
        kpos = (j * tk + lax.broadcasted_iota(jnp.int32, (1, tk), 1)).astype(F32)
        bias = slope * jnp.abs(qpos - kpos)
        for c in range(2):
            s = lax.dot_general(qh[c], kb, NT_DIMS, preferred_element_type=F32) - bias
            m_old = m_ref[c]
            m_new = jnp.maximum(m_old, jnp.max(s, axis=-1, keepdims=True))
            p = jnp.exp(s - m_new)
            alpha = jnp.exp(m_old - m_new)
            l_ref[c] = alpha * l_ref[c] + jnp.sum(p, axis=-1, keepdims=True)
            acc_ref[c] = alpha * acc_ref[c] + jnp.dot(p.astype(BF16), vb, preferred_element_type=F32)
            m_ref[c] = m_new
        return carry

    lax.fori_loop(0, s_len // tk, body, 0)
    o = acc_ref[0] / l_ref[0] - lam * (acc_ref[1] / l_ref[1])
    y = o * lax.rsqrt(jnp.mean(o * o, axis=-1, keepdims=True) + RMS_EPS) * g_ref[...]
    o_ref[...] = (y * (1.0 - linit)).astype(o_ref.dtype)


def _attention(q, k, v, lamv, slopes, subln_g, linit, tq=256, tk=512):
    b, s, aw = q.shape
    hv = ATTN_HEAD_V
    qspec = pl.BlockSpec((None, tq, hv), lambda bi, hi, qi: (bi, qi, hi))
    kvspec = pl.BlockSpec((None, s, hv), lambda bi, hi, qi: (bi, 0, hi))
    return pl.pallas_call(
        functools.partial(_attn_kernel, tq=tq, tk=tk, linit=linit),
        grid=(b, aw // hv, s // tq),
        in_specs=[pl.BlockSpec((4, ATTN_HEAD_QK), lambda bi, hi, qi: (0, 0)),
                  pl.BlockSpec(memory_space=pltpu.SMEM),
                  qspec, kvspec, kvspec,
                  pl.BlockSpec((1, hv), lambda bi, hi, qi: (0, 0))],
        out_specs=qspec,
        out_shape=jax.ShapeDtypeStruct((b, s, aw), BF16),
        scratch_shapes=[pltpu.VMEM((2, tq, 1), F32), pltpu.VMEM((2, tq, 1), F32),
                        pltpu.VMEM((2, tq, hv), F32)],
        compiler_params=_params(("parallel", "parallel", "arbitrary")),
        name="attn",
    )(lamv, slopes, q, k, v, subln_g)


def _fourier_kernel(c_ref, ns_ref, a_ref, b_ref, bias_ref, f_ref, acc_ref):
    kk = pl.program_id(2)

    @pl.when(kk == 0)
    def _():
        acc_ref[...] = jnp.zeros(acc_ref.shape, F32)

    acc_ref[...] += (jnp.dot(c_ref[...], a_ref[...], preferred_element_type=F32)
                     + jnp.dot(ns_ref[...], b_ref[...], preferred_element_type=F32))

    @pl.when(kk == pl.num_programs(2) - 1)
    def _():
        f_ref[...] = (acc_ref[...] + bias_ref[...]).astype(f_ref.dtype)


def _dft_matrices(s):
    idx = jnp.arange(s, dtype=jnp.int32)
    ang = ((idx[:, None] * idx[None, :]) % s).astype(F32) * (2.0 * math.pi / s)
    scale = s ** -0.5
    return (jnp.cos(ang) * scale).astype(BF16), (jnp.sin(ang) * (-scale)).astype(BF16)


def _fourier(a, bm, bias, tm=1024, tk=1024):
    b, s, fw = a.shape
    cmat, nsmat = _dft_matrices(s)
    mspec = pl.BlockSpec((tm, tk), lambda bi, i, kk: (i, kk))
    xspec = pl.BlockSpec((None, tk, fw), lambda bi, i, kk: (bi, kk, 0))
    return pl.pallas_call(
        _fourier_kernel,
        grid=(b, s // tm, s // tk),
        in_specs=[mspec, mspec, xspec, xspec, pl.BlockSpec((1, fw), lambda bi, i, kk: (0, 0))],
        out_specs=pl.BlockSpec((None, tm, fw), lambda bi, i, kk: (bi, i, 0)),
        out_shape=jax.ShapeDtypeStruct((b, s, fw), BF16),
        scratch_shapes=[pltpu.VMEM((tm, fw), F32)],
        compiler_params=_params(("parallel", "parallel", "arbitrary")),
        name="fourier",
    )(cmat, nsmat, a, bm, bias)


def _outproj_kernel(o_ref, f_ref, x_ref, gi_ref, bi_ref, wo_ref, g1_ref, b1_ref, wq_ref,
                    x1_ref, qp_ref, *, alpha):
    aw = o_ref.shape[1]
    mix = (jnp.dot(o_ref[...], wo_ref[:aw, :], preferred_element_type=F32)
           + jnp.dot(f_ref[...], wo_ref[aw:, :], preferred_element_type=F32))
    xn = _layer_norm(x_ref[...], gi_ref[...], bi_ref[...])
    x1 = _layer_norm(alpha * xn + mix, g1_ref[...], b1_ref[...])
    x1_ref[...] = x1
    qp_ref[...] = jnp.dot(x1.astype(BF16), wq_ref[...], preferred_element_type=F32)


def _out_proj(o2, f2, x2, gi, bi, wo, g1, b1, wq, alpha, tm=512):
    t, d = x2.shape
    row = lambda i: (i, 0)
    full = lambda i: (0, 0)
    vec = pl.BlockSpec((1, d), full)
    return pl.pallas_call(
        functools.partial(_outproj_kernel, alpha=alpha),
        grid=(t // tm,),
        in_specs=[pl.BlockSpec((tm, o2.shape[1]), row), pl.BlockSpec((tm, f2.shape[1]), row),
                  pl.BlockSpec((tm, d), row), vec, vec, pl.BlockSpec(wo.shape, full), vec, vec,
                  pl.BlockSpec(wq.shape, full)],
        out_specs=[pl.BlockSpec((tm, d), row), pl.BlockSpec((tm, wq.shape[1]), row)],
        out_shape=[jax.ShapeDtypeStruct((t, d), F32), jax.ShapeDtypeStruct((t, wq.shape[1]), F32)],
        compiler_params=_params(("parallel",)),
        name="out_proj",
    )(o2, f2, x2, gi, bi, wo, g1, b1, wq)


def _topk_cols(s, pos, k, payload=None):
    big = jnp.float32(1e9)
    vals, sels = [], []
    for _ in range(k):
        m = jnp.max(s, axis=0, keepdims=True)
        am = jnp.min(jnp.where(s == m, pos, big), axis=0, keepdims=True)
        hit = pos == am
        vals.append(m)
        if payload is None:
            sels.append(am)
        else:
            sels.append(jnp.max(jnp.where(hit, payload, -1.0), axis=0, keepdims=True))
        s = jnp.where(hit, -jnp.inf, s)
    return vals, sels


def _pair_candidates(s1r, i1r, s2r, i2r):
    kk = PEER_TOPK
    tb = s1r[0].shape[1]
    r16 = lax.broadcasted_iota(jnp.int32, (kk, tb), 0).astype(F32)
    r8 = lax.broadcasted_iota(jnp.int32, (SUBLANES, tb), 0).astype(F32)
    nk = float(PEER_N_KEYS)
    ninf = -jnp.inf
    cat = lambda rows: jnp.concatenate(rows, axis=0)
    s1, i1, s2, i2 = cat(s1r), cat(i1r), cat(s2r), cat(i2r)
    s1l, i1l, s2l, i2l = (cat(r[:SUBLANES]) for r in (s1r, i1r, s2r, i2r))
    groups = []
    groups.append((s1 + s2r[0], kk * r16, i1 * nk + i2r[0], None))
    groups.append((s1r[0] + s2, r16, i1r[0] * nk + i2, r16 >= 1.0))
    groups.append((s1l + s2r[1], kk * r8 + 1.0, i1l * nk + i2r[1], r8 >= 1.0))
    groups.append((s1r[1] + s2l, kk + r8, i1r[1] * nk + i2l, r8 >= 2.0))
    for b in (2, 3, 4):
        amax = kk // (b + 1) - 1
        groups.append((s1l + s2r[b], kk * r8 + float(b), i1l * nk + i2r[b],
                       (r8 >= 2.0) & (r8 <= float(amax))))
    val = jnp.concatenate([v if ok is None else jnp.where(ok, v, ninf) for v, _, _, ok in groups], axis=0)
    pos = jnp.concatenate([p for _, p, _, _ in groups], axis=0)
    eid = jnp.concatenate([e for _, _, e, _ in groups], axis=0)
    return val, pos, eid


def _topk_kernel(q_ref, sk_ref, e_ref, g_ref):
    q = q_ref[...].astype(BF16)
    tb = q.shape[0]
    key_pos = lax.broadcasted_iota(jnp.int32, (PEER_N_KEYS, tb), 0).astype(F32)
    tops = []
    for c in range(2):
        sc = lax.dot_general(sk_ref[c], q, NT_DIMS, preferred_element_type=F32)
        tops.append(_topk_cols(sc, key_pos, PEER_TOPK))
    (s1, i1), (s2, i2) = tops
    val, pos, eid = _pair_candidates(s1, i1, s2, i2)
    bestr, eselr = _topk_cols(val, pos, PEER_TOPK, payload=eid)
    best = jnp.concatenate(bestr, axis=0)
    ex = jnp.exp(best - bestr[0])
    e_ref[...] = jnp.concatenate(eselr, axis=0).astype(jnp.int32)
    g_ref[...] = ex / jnp.sum(ex, axis=0, keepdims=True)


def _topk(qp, sk_ext, tb=256):
    t = qp.shape[0]
    hp = PEER_HEADS
    ospec = pl.BlockSpec((None, PEER_TOPK, tb), lambda i, h: (h, 0, i))
    return pl.pallas_call(
        _topk_kernel,
        grid=(t // tb, hp),
        in_specs=[pl.BlockSpec((tb, LANES), lambda i, h: (i, h)),
                  pl.BlockSpec((None, 2, PEER_N_KEYS, LANES), lambda i, h: (h, 0, 0, 0))],
        out_specs=[ospec, ospec],
        out_shape=[jax.ShapeDtypeStruct((hp, PEER_TOPK, t), jnp.int32),
                   jax.ShapeDtypeStruct((hp, PEER_TOPK, t), F32)],
        compiler_params=_params(("parallel", "parallel")),
        name="topk",
    )(qp, sk_ext)


HALF_ROWS = 4


def _pack_table(w):
    n, d = w.shape
    bits = lax.bitcast_convert_type(w.astype(BF16), jnp.uint16).astype(jnp.uint32)
    lo = bits[:, :d // 2].reshape(n, HALF_ROWS, LANES)
    hi = bits[:, d // 2:].reshape(n, HALF_ROWS, LANES)
    return lo | (hi << 16)


def _gather_pairs(tab_ref, eidx_ref, base, n_pairs):
    idx = eidx_ref.at[pl.ds(base, 2 * n_pairs)]
    out = []
    for p in range(n_pairs):
        wa = tab_ref[idx[2 * p]]
        wb = tab_ref[idx[2 * p + 1]]
        out.append(pltpu.bitcast(jnp.concatenate([wa, wb], axis=0), BF16))
    return out


TOK_UNROLL = 8


U_BLOCK = 256


def _peer_u_kernel(x_ref, e_ref, g_ref, ut_ref, coef_ref, xb_ref, blk_ref, lane_ref, act_ref, s_ref):
    xb_ref[...] = x_ref[...].astype(BF16)
    e = e_ref[...]
    blk_ref[...] = e // LANES
    lane_ref[...] = e % LANES
    act_ref[...] = jnp.zeros(act_ref.shape, F32)
    sub = U_BLOCK // LANES
    nblk = ut_ref.shape[0]

    def score(i, slot):
        s_ref[slot] = jnp.dot(xb_ref[...], ut_ref[i], preferred_element_type=F32)

    def gather(i, slot):
        lane = lane_ref[...]
        blk = blk_ref[...]
        act = act_ref[...]
        for h in range(sub):
            got = jnp.take_along_axis(s_ref[slot, :, h * LANES:(h + 1) * LANES], lane, axis=1)
            act = jnp.where(blk == i * sub + h, got, act)
        act_ref[...] = act

    score(0, 0)

    def body(i, carry):
        score(2 * i + 1, 1)
        gather(2 * i, 0)
        score(2 * i + 2, 0)
        gather(2 * i + 1, 1)
        return carry

    lax.fori_loop(0, nblk // 2 - 1, body, 0)
    score(nblk - 1, 1)
    gather(nblk - 2, 0)
    gather(nblk - 1, 1)
    coef_ref[...] = g_ref[...] * jax.nn.gelu(act_ref[...])


def _peer_u(eidx, x1, gates, ut, tb=1024):
    t, d = x1.shape
    row = lambda i: (i, 0)
    return pl.pallas_call(
        _peer_u_kernel,
        grid=(t // tb,),
        in_specs=[pl.BlockSpec((tb, d), row), pl.BlockSpec((tb, PICKS), row),
                  pl.BlockSpec((tb, PICKS), row), pl.BlockSpec(memory_space=pltpu.VMEM)],
        out_specs=pl.BlockSpec((tb, PICKS), row),
        out_shape=jax.ShapeDtypeStruct((t, PICKS), F32),
        scratch_shapes=[pltpu.VMEM((tb, d), BF16),
                        pltpu.VMEM((tb, PICKS), jnp.int32), pltpu.VMEM((tb, PICKS), jnp.int32),
                        pltpu.VMEM((tb, PICKS), F32), pltpu.VMEM((2, tb, U_BLOCK), F32)],
        compiler_params=_params(("arbitrary",)),
        name="peer_u",
    )(x1, eidx, gates, ut)


def _peer_v_kernel(eidx_ref, coef_ref, x_ref, g_ref, b_ref, tab_ref, y_ref, crep, obuf, *, alpha):
    tb, d = x_ref.shape
    nblk = d // LANES
    rep = d // PICKS
    expand = (lax.broadcasted_iota(jnp.int32, (PICKS, d), 1) // rep
              == lax.broadcasted_iota(jnp.int32, (PICKS, d), 0)).astype(BF16)
    crep[...] = jnp.dot(coef_ref[...].astype(BF16), expand, preferred_element_type=F32)
    jj = lax.broadcasted_iota(jnp.int32, (nblk, d), 1) % rep
    mask = (jj % 2) * HALF_ROWS + jj // 2 == lax.broadcasted_iota(jnp.int32, (nblk, d), 0)

    def body(grp, carry):
        cg = crep[pl.ds(pl.multiple_of(grp * TOK_UNROLL, TOK_UNROLL), TOK_UNROLL), :]
        for tt in range(TOK_UNROLL):
            t = grp * TOK_UNROLL + tt
            lhs = jnp.where(mask, jnp.broadcast_to(cg[tt:tt + 1, :], (nblk, d)), 0.0).astype(BF16)
            tiles = _gather_pairs(tab_ref, eidx_ref, t * PICKS, PICKS // 2)
            rhs = jnp.concatenate(tiles, axis=0)
            obuf[pl.ds(pl.multiple_of(t * nblk, nblk), nblk), :] = jnp.dot(
                lhs, rhs, preferred_element_type=F32)
        return carry

    lax.fori_loop(0, tb // TOK_UNROLL, body, 0)
    peer = jnp.concatenate([obuf[pl.ds(j, tb, stride=nblk), :] for j in range(nblk)], axis=1)
    y_ref[...] = _layer_norm(alpha * x_ref[...] + peer, g_ref[...], b_ref[...])


def _peer_v(eidx_flat, coef, x1, g2, b2, tab, alpha, tb=256):
    t, d = x1.shape
    row = lambda i: (i, 0)
    vec = pl.BlockSpec((1, d), lambda i: (0, 0))
    return pl.pallas_call(
        functools.partial(_peer_v_kernel, alpha=alpha),
        grid=(t // tb,),
        in_specs=[pl.BlockSpec((tb * PICKS,), lambda i: (i,), memory_space=pltpu.SMEM),
                  pl.BlockSpec((tb, PICKS), row), pl.BlockSpec((tb, d), row), vec, vec,
                  pl.BlockSpec(memory_space=pltpu.VMEM)],
        out_specs=pl.BlockSpec((tb, d), row),
        out_shape=jax.ShapeDtypeStruct((t, d), F32),
        scratch_shapes=[pltpu.VMEM((tb, d), F32), pltpu.VMEM((tb * (d // LANES), LANES), F32)],
        compiler_params=_params(("arbitrary",)),
        name="peer_v",
    )(eidx_flat, coef, x1, g2, b2, tab)


def _trunk(x, p):
    b, s, d = x.shape
    t = b * s
    depth = p["w_in"].shape[0]
    alpha = (2 * depth) ** 0.25
    x2 = x.reshape(t, d)
    cur = x2
    gi, bi = p["ln_in_g"].reshape(1, d), p["ln_in_b"].reshape(1, d)
    for l in range(depth):
        assert l == 0, "single-layer trunk"
        linit = 0.8 - 0.6 * math.exp(-0.3 * l)
        aw = N_ATTN_HEADS * ATTN_HEAD_V
        w_in = p["w_in"][l]
        wab = _prep(w_in[:, 3 * aw:], p["w_fourier"][l])
        q, k, v, a, bm = _in_proj(cur, gi, bi, w_in[:, :3 * aw].astype(BF16), wab)
        lamv = jnp.stack([p["lambda_q1"][l], p["lambda_k1"][l], p["lambda_q2"][l], p["lambda_k2"][l]])
        hh = jnp.arange(1, N_ATTN_HEADS + 1, dtype=F32)
        slopes = jnp.exp2(-8.0 * hh / N_ATTN_HEADS)
        o = _attention(q.reshape(b, s, aw), k.reshape(b, s, aw), v.reshape(b, s, aw), lamv, slopes,
                       p["subln_g"][l].reshape(1, ATTN_HEAD_V), linit)
        f = _fourier(a.reshape(b, s, -1), bm.reshape(b, s, -1), p["b_fourier"][l].reshape(1, -1))
        x1, qp = _out_proj(o.reshape(t, aw), f.reshape(t, -1), cur, gi, bi,
                           p["w_out"][l].astype(BF16), p["ln1_g"][l].reshape(1, d),
                           p["ln1_b"][l].reshape(1, d), p["peer_wq"][l].astype(BF16), alpha)
        sk = p["peer_subkeys"][l]
        z = jnp.zeros_like(sk)
        sk_ext = jnp.stack([jnp.concatenate([sk[:, 0], z[:, 0]], axis=-1),
                            jnp.concatenate([z[:, 1], sk[:, 1]], axis=-1)], axis=1).astype(BF16)
        e_t, g_t = _topk(qp, sk_ext)
        eidx = e_t.reshape(PICKS, t).T
        gates = g_t.reshape(PICKS, t).T
        u_tab = p["peer_u"][l].astype(BF16)
        ut = u_tab.reshape(u_tab.shape[0] // U_BLOCK, U_BLOCK, d).transpose(0, 2, 1)
        coef = _peer_u(eidx, x1, gates, ut)
        cur = _peer_v(eidx.reshape(t * PICKS), coef, x1, p["ln2_g"][l].reshape(1, d),
                      p["ln2_b"][l].reshape(1, d), _pack_table(p["peer_v"][l]), alpha)
    return cur.reshape(b, s, d)


def kernel(x_prompt, x_sample, ln_in_g, ln_in_b, w_in, lambda_q1, lambda_k1, lambda_q2, lambda_k2,
           subln_g, w_fourier, b_fourier, w_out, ln1_g, ln1_b, peer_wq, peer_subkeys, peer_u, peer_v,
           ln2_g, ln2_b):
    p = dict(ln_in_g=ln_in_g, ln_in_b=ln_in_b, w_in=w_in, lambda_q1=lambda_q1, lambda_k1=lambda_k1,
             lambda_q2=lambda_q2, lambda_k2=lambda_k2, subln_g=subln_g, w_fourier=w_fourier,
             b_fourier=b_fourier, w_out=w_out, ln1_g=ln1_g, ln1_b=ln1_b, peer_wq=peer_wq,
             peer_subkeys=peer_subkeys, peer_u=peer_u, peer_v=peer_v, ln2_g=ln2_g, ln2_b=ln2_b)
    return (_trunk(x_prompt, p), _trunk(x_sample, p))
```

```python
import functools
import math

import jax
import jax.numpy as jnp
from jax import lax
from jax.experimental import pallas as pl
from jax.experimental.pallas import tpu as pltpu

F32 = jnp.float32
BF16 = jnp.bfloat16

LN_EPS = 1e-5
RMS_EPS = 1e-5
N_ATTN_HEADS = 4
ATTN_HEAD_V = 128
ATTN_HEAD_QK = 64
N_FOURIER_GROUPS = 4
FOURIER_GROUP = 128
PEER_HEADS = 8
PEER_N_KEYS = 128
PEER_HALF = 64
PEER_TOPK = 16
PICKS = PEER_HEADS * PEER_TOPK

LANES = 128
SUBLANES = 8
VMEM_LIMIT = 56 * 1024 * 1024
PEER_V_VMEM_LIMIT = 60 * 1024 * 1024

NT_DIMS = (((1,), (1,)), ((), ()))


def _params(sem, vmem=VMEM_LIMIT):
    return pltpu.CompilerParams(dimension_semantics=sem, vmem_limit_bytes=vmem)


def _layer_norm(x, g, b):
    mu = jnp.mean(x, axis=-1, keepdims=True)
    xc = x - mu
    var = jnp.mean(xc * xc, axis=-1, keepdims=True)
    return xc * lax.rsqrt(var + LN_EPS) * g + b


def _prep_kernel(win_f_ref, wf_ref, c_ref, s_ref, wab_ref):
    hp = lax.Precision.HIGHEST
    for g in range(N_FOURIER_GROUPS):
        wg = wf_ref[g]
        cw = jnp.dot(c_ref[...], wg, precision=hp, preferred_element_type=F32)
        sw = jnp.dot(s_ref[...], wg, precision=hp, preferred_element_type=F32)
        wi = win_f_ref[:, g * FOURIER_GROUP:(g + 1) * FOURIER_GROUP]
        a = jnp.dot(wi, cw, precision=hp, preferred_element_type=F32)
        b = jnp.dot(wi, sw, precision=hp, preferred_element_type=F32)
        wab_ref[:, g * FOURIER_GROUP:(g + 1) * FOURIER_GROUP] = a.astype(BF16)
        wab_ref[:, (N_FOURIER_GROUPS + g) * FOURIER_GROUP:
                (N_FOURIER_GROUPS + g + 1) * FOURIER_GROUP] = b.astype(BF16)


def _prep(win_f, w_fourier):
    d = win_f.shape[0]
    n = FOURIER_GROUP
    idx = jnp.arange(n, dtype=jnp.int32)
    ang = ((idx[:, None] * idx[None, :]) % n).astype(F32) * (2.0 * math.pi / n)
    cmat = jnp.cos(ang) * (n ** -0.5)
    smat = jnp.sin(ang) * (n ** -0.5)
    return pl.pallas_call(
        _prep_kernel,
        out_shape=jax.ShapeDtypeStruct((d, 2 * N_FOURIER_GROUPS * n), BF16),
        name="prep",
    )(win_f, w_fourier, cmat, smat)


def _inproj_kernel(x_ref, g_ref, b_ref, wqkv_ref, wab_ref, q_ref, k_ref, v_ref, a_ref, bb_ref):
    xn = _layer_norm(x_ref[...], g_ref[...], b_ref[...]).astype(BF16)
    h = jnp.dot(xn, wqkv_ref[...], preferred_element_type=F32)
    aw = q_ref.shape[1]
    q_ref[...] = (h[:, :aw] * (ATTN_HEAD_QK ** -0.5)).astype(BF16)
    k_ref[...] = h[:, aw:2 * aw].astype(BF16)
    v_ref[...] = h[:, 2 * aw:3 * aw].astype(BF16)
    ab = jnp.dot(xn, wab_ref[...], preferred_element_type=F32)
    fw = a_ref.shape[1]
    a_ref[...] = ab[:, :fw].astype(BF16)
    bb_ref[...] = ab[:, fw:].astype(BF16)


def _in_proj(x2, g, b, wqkv, wab, tm=512):
    t, d = x2.shape
    aw = wqkv.shape[1] // 3
    fw = wab.shape[1] // 2
    row = lambda i: (i, 0)
    full = lambda i: (0, 0)
    outs = [jax.ShapeDtypeStruct((t, aw), BF16)] * 3 + [jax.ShapeDtypeStruct((t, fw), BF16)] * 2
    return pl.pallas_call(
        _inproj_kernel,
        grid=(t // tm,),
        in_specs=[pl.BlockSpec((tm, d), row), pl.BlockSpec((1, d), full), pl.BlockSpec((1, d), full),
                  pl.BlockSpec(wqkv.shape, full), pl.BlockSpec(wab.shape, full)],
        out_specs=[pl.BlockSpec((tm, aw), row)] * 3 + [pl.BlockSpec((tm, fw), row)] * 2,
        out_shape=outs,
        compiler_params=_params(("parallel",)),
        name="in_proj",
    )(x2, g, b, wqkv, wab)


def _attn_kernel(lamv_ref, slope_ref, q_ref, k_ref, v_ref, g_ref, o_ref,
                 ka_ref, s_ref, m_ref, l_ref, acc_ref, *, tq, tk, linit):
    h = pl.program_id(1)
    i = pl.program_id(2)
    s_len = k_ref.shape[0]
    nkv = s_len // tk
    nl = tk // LANES
    slope = slope_ref[h]
    lv = lamv_ref[...]
    lam = (jnp.exp(jnp.sum(lv[0:1] * lv[1:2], axis=-1, keepdims=True))
           - jnp.exp(jnp.sum(lv[2:3] * lv[3:4], axis=-1, keepdims=True)) + linit)
    dq = ATTN_HEAD_QK

    def aug_lanes(shape, first_pos, c):
        lane = lax.broadcasted_iota(jnp.int32, shape, 1)
        pos = first_pos + lax.broadcasted_iota(jnp.int32, shape, 0)
        hi = (pos // dq).astype(F32) * (slope * dq)
        lo = (pos % dq).astype(F32) * slope
        base = dq * (1 - c)
        keep = (lane < dq) if c == 0 else (lane >= dq)
        return lane - base, hi, lo, keep

    @pl.when(i == 0)
    def _():
        k = k_ref[...]
        for c in range(2):
            rel, hi, lo, keep = aug_lanes(k.shape, 0, c)
            aug = jnp.where(rel == 0, hi, jnp.where(rel == 1, lo,
                            jnp.where((rel == 2) | (rel == 3), 1.0, 0.0)))
            ka_ref[c] = jnp.where(keep, k, aug.astype(BF16))

    q = q_ref[...]
    q_left, q_right, q_diag = [], [], []
    for c in range(2):
        rel, hi, lo, keep = aug_lanes(q.shape, i * tq, c)
        aug = jnp.where((rel == 0) | (rel == 1), 1.0,
                        jnp.where(rel == 2, -hi, jnp.where(rel == 3, -lo, 0.0)))
        q_left.append(jnp.where(keep, q, aug.astype(BF16)))
        q_right.append(jnp.where(keep, q, (-aug).astype(BF16)))
        q_diag.append(jnp.where(keep, q, jnp.zeros_like(q)))

    m_ref[...] = jnp.full(m_ref.shape, -jnp.inf, F32)

    def scores(j, qv, bias):
        start = pl.multiple_of(j * tk, tk)
        for c in range(2):
            s = lax.dot_general(qv[c], ka_ref[c, pl.ds(start, tk), :], NT_DIMS,
                                preferred_element_type=F32)
            if bias is not None:
                s = s - bias
            s_ref[c, j] = s
            mt = m_ref[c]
            for l in range(nl):
                mt = jnp.maximum(mt, s[:, l * LANES:(l + 1) * LANES])
            m_ref[c] = mt

    jd = (i * tq) // tk
    lax.fori_loop(0, jd, lambda j, cr: (scores(j, q_left, None), cr)[1], 0)
    qpos = (i * tq + lax.broadcasted_iota(jnp.int32, (tq, 1), 0)).astype(F32)
    kpos = (jd * tk + lax.broadcasted_iota(jnp.int32, (1, tk), 1)).astype(F32)
    scores(jd, q_diag, slope * jnp.abs(qpos - kpos))
    lax.fori_loop(jd + 1, nkv, lambda j, cr: (scores(j, q_right, None), cr)[1], 0)

    for c in range(2):
        m_ref[c] = jnp.broadcast_to(jnp.max(m_ref[c], axis=-1, keepdims=True), (tq, LANES))
    l_ref[...] = jnp.zeros(l_ref.shape, F32)
    acc_ref[...] = jnp.zeros(acc_ref.shape, F32)

    def accumulate(j, carry):
        vb = v_ref[pl.ds(pl.multiple_of(j * tk, tk), tk), :]
        for c in range(2):
            s = s_ref[c, j]
            mb = m_ref[c]
            ps = [jnp.exp(s[:, l * LANES:(l + 1) * LANES] - mb) for l in range(nl)]
            lsum = l_ref[c]
            for p in ps:
                lsum = lsum + p
            l_ref[c] = lsum
            acc_ref[c] += jnp.dot(jnp.concatenate(ps, axis=1).astype(BF16), vb,
                                  preferred_element_type=F32)
        return carry

    lax.fori_loop(0, nkv, accumulate, 0)
    l0 = jnp.sum(l_ref[0], axis=-1, keepdims=True)
    l1 = jnp.sum(l_ref[1], axis=-1, keepdims=True)
    o = acc_ref[0] / l0 - lam * (acc_ref[1] / l1)
    y = o * lax.rsqrt(jnp.mean(o * o, axis=-1, keepdims=True) + RMS_EPS) * g_ref[...]
    o_ref[...] = (y * (1.0 - linit)).astype(o_ref.dtype)


def _attention(q, k, v, lamv, slopes, subln_g, linit, tq=256, tk=1024):
    b, s, aw = q.shape
    hv = ATTN_HEAD_V
    assert tk % tq == 0 and s % tk == 0
    qspec = pl.BlockSpec((None, tq, hv), lambda bi, hi, qi: (bi, qi, hi))
    kvspec = pl.BlockSpec((None, s, hv), lambda bi, hi, qi: (bi, 0, hi))
    return pl.pallas_call(
        functools.partial(_attn_kernel, tq=tq, tk=tk, linit=linit),
        grid=(b, aw // hv, s // tq),
        in_specs=[pl.BlockSpec((4, ATTN_HEAD_QK), lambda bi, hi, qi: (0, 0)),
                  pl.BlockSpec(memory_space=pltpu.SMEM),
                  qspec, kvspec, kvspec,
                  pl.BlockSpec((1, hv), lambda bi, hi, qi: (0, 0))],
        out_specs=qspec,
        out_shape=jax.ShapeDtypeStruct((b, s, aw), BF16),
        scratch_shapes=[pltpu.VMEM((2, s, hv), BF16), pltpu.VMEM((2, s // tk, tq, tk), F32),
                        pltpu.VMEM((2, tq, LANES), F32), pltpu.VMEM((2, tq, LANES), F32),
                        pltpu.VMEM((2, tq, hv), F32)],
        compiler_params=_params(("parallel", "parallel", "arbitrary")),
        name="attn",
    )(lamv, slopes, q, k, v, subln_g)


def _fourier_kernel(c_ref, ns_ref, a_ref, b_ref, bias_ref, f_ref, acc_ref):
    kk = pl.program_id(2)

    @pl.when(kk == 0)
    def _():
        acc_ref[...] = jnp.zeros(acc_ref.shape, F32)

    acc_ref[...] += (jnp.dot(c_ref[...], a_ref[...], preferred_element_type=F32)
                     + jnp.dot(ns_ref[...], b_ref[...], preferred_element_type=F32))

    @pl.when(kk == pl.num_programs(2) - 1)
    def _():
        f_ref[...] = (acc_ref[...] + bias_ref[...]).astype(f_ref.dtype)


def _dft_matrices(s):
    idx = jnp.arange(s, dtype=jnp.int32)
    ang = ((idx[:, None] * idx[None, :]) % s).astype(F32) * (2.0 * math.pi / s)
    scale = s ** -0.5
    return (jnp.cos(ang) * scale).astype(BF16), (jnp.sin(ang) * (-scale)).astype(BF16)


def _fourier(a, bm, bias, tm=1024, tk=1024):
    b, s, fw = a.shape
    cmat, nsmat = _dft_matrices(s)
    mspec = pl.BlockSpec((tm, tk), lambda bi, i, kk: (i, kk))
    xspec = pl.BlockSpec((None, tk, fw), lambda bi, i, kk: (bi, kk, 0))
    return pl.pallas_call(
        _fourier_kernel,
        grid=(b, s // tm, s // tk),
        in_specs=[mspec, mspec, xspec, xspec, pl.BlockSpec((1, fw), lambda bi, i, kk: (0, 0))],
        out_specs=pl.BlockSpec((None, tm, fw), lambda bi, i, kk: (bi, i, 0)),
        out_shape=jax.ShapeDtypeStruct((b, s, fw), BF16),
        scratch_shapes=[pltpu.VMEM((tm, fw), F32)],
        compiler_params=_params(("parallel", "parallel", "arbitrary")),
        name="fourier",
    )(cmat, nsmat, a, bm, bias)


def _outproj_kernel(o_ref, f_ref, x_ref, gi_ref, bi_ref, wo_ref, g1_ref, b1_ref, wq_ref,
                    x1_ref, qp_ref, *, alpha):
    aw = o_ref.shape[1]
    mix = (jnp.dot(o_ref[...], wo_ref[:aw, :], preferred_element_type=F32)
           + jnp.dot(f_ref[...], wo_ref[aw:, :], preferred_element_type=F32))
    xn = _layer_norm(x_ref[...], gi_ref[...], bi_ref[...])
    x1 = _layer_norm(alpha * xn + mix, g1_ref[...], b1_ref[...])
    x1_ref[...] = x1
    qp_ref[...] = jnp.dot(x1.astype(BF16), wq_ref[...], preferred_element_type=F32)


def _out_proj(o2, f2, x2, gi, bi, wo, g1, b1, wq, alpha, tm=512):
    t, d = x2.shape
    row = lambda i: (i, 0)
    full = lambda i: (0, 0)
    vec = pl.BlockSpec((1, d), full)
    return pl.pallas_call(
        functools.partial(_outproj_kernel, alpha=alpha),
        grid=(t // tm,),
        in_specs=[pl.BlockSpec((tm, o2.shape[1]), row), pl.BlockSpec((tm, f2.shape[1]), row),
                  pl.BlockSpec((tm, d), row), vec, vec, pl.BlockSpec(wo.shape, full), vec, vec,
                  pl.BlockSpec(wq.shape, full)],
        out_specs=[pl.BlockSpec((tm, d), row), pl.BlockSpec((tm, wq.shape[1]), row)],
        out_shape=[jax.ShapeDtypeStruct((t, d), F32), jax.ShapeDtypeStruct((t, wq.shape[1]), F32)],
        compiler_params=_params(("parallel",)),
        name="out_proj",
    )(o2, f2, x2, gi, bi, wo, g1, b1, wq)


def _topk_cols(s, pos, k, payload=None):
    big = jnp.float32(1e9)
    vals, sels = [], []
    for _ in range(k):
        m = jnp.max(s, axis=0, keepdims=True)
        am = jnp.min(jnp.where(s == m, pos, big), axis=0, keepdims=True)
        hit = pos == am
        vals.append(m)
        if payload is None:
            sels.append(am)
        else:
            sels.append(jnp.max(jnp.where(hit, payload, -1.0), axis=0, keepdims=True))
        s = jnp.where(hit, -jnp.inf, s)
    return vals, sels


def _pair_candidates(s1r, i1r, s2r, i2r):
    kk = PEER_TOPK
    tb = s1r[0].shape[1]
    r16 = lax.broadcasted_iota(jnp.int32, (kk, tb), 0).astype(F32)
    r8 = lax.broadcasted_iota(jnp.int32, (SUBLANES, tb), 0).astype(F32)
    nk = float(PEER_N_KEYS)
    ninf = -jnp.inf
    cat = lambda rows: jnp.concatenate(rows, axis=0)
    s1, i1, s2, i2 = cat(s1r), cat(i1r), cat(s2r), cat(i2r)
    s1l, i1l, s2l, i2l = (cat(r[:SUBLANES]) for r in (s1r, i1r, s2r, i2r))
    groups = []
    groups.append((s1 + s2r[0], kk * r16, i1 * nk + i2r[0], None))
    groups.append((s1r[0] + s2, r16, i1r[0] * nk + i2, r16 >= 1.0))
    groups.append((s1l + s2r[1], kk * r8 + 1.0, i1l * nk + i2r[1], r8 >= 1.0))
    groups.append((s1r[1] + s2l, kk + r8, i1r[1] * nk + i2l, r8 >= 2.0))
    for b in (2, 3, 4):
        amax = kk // (b + 1) - 1
        groups.append((s1l + s2r[b], kk * r8 + float(b), i1l * nk + i2r[b],
                       (r8 >= 2.0) & (r8 <= float(amax))))
    val = jnp.concatenate([v if ok is None else jnp.where(ok, v, ninf) for v, _, _, ok in groups], axis=0)
    pos = jnp.concatenate([p for _, p, _, _ in groups], axis=0)
    eid = jnp.concatenate([e for _, _, e, _ in groups], axis=0)
    return val, pos, eid


def _topk_kernel(q_ref, sk_ref, e_ref, g_ref):
    q = q_ref[...].astype(BF16)
    tb = q.shape[0]
    key_pos = lax.broadcasted_iota(jnp.int32, (PEER_N_KEYS, tb), 0).astype(F32)
    tops = []
    for c in range(2):
        sc = lax.dot_general(sk_ref[c], q, NT_DIMS, preferred_element_type=F32)
        tops.append(_topk_cols(sc, key_pos, PEER_TOPK))
    (s1, i1), (s2, i2) = tops
    val, pos, eid = _pair_candidates(s1, i1, s2, i2)
    bestr, eselr = _topk_cols(val, pos, PEER_TOPK, payload=eid)
    best = jnp.concatenate(bestr, axis=0)
    ex = jnp.exp(best - bestr[0])
    e_ref[...] = jnp.concatenate(eselr, axis=0).astype(jnp.int32)
    g_ref[...] = ex / jnp.sum(ex, axis=0, keepdims=True)


def _topk(qp, sk_ext, tb=256):
    t = qp.shape[0]
    hp = PEER_HEADS
    ospec = pl.BlockSpec((None, PEER_TOPK, tb), lambda i, h: (h, 0, i))
    return pl.pallas_call(
        _topk_kernel,
        grid=(t // tb, hp),
        in_specs=[pl.BlockSpec((tb, LANES), lambda i, h: (i, h)),
                  pl.BlockSpec((None, 2, PEER_N_KEYS, LANES), lambda i, h: (h, 0, 0, 0))],
        out_specs=[ospec, ospec],
        out_shape=[jax.ShapeDtypeStruct((hp, PEER_TOPK, t), jnp.int32),
                   jax.ShapeDtypeStruct((hp, PEER_TOPK, t), F32)],
        compiler_params=_params(("parallel", "parallel")),
        name="topk",
    )(qp, sk_ext)


TOK_UNROLL = 8
U_BLOCK = 256


V_BLOCK = 1024


def _block_transposed(table, block):
    n, d = table.shape
    return table.astype(BF16).reshape(n // block, block, d).transpose(0, 2, 1)


def _peer_u_kernel(x_ref, e_ref, g_ref, ut_ref, coef_ref, xb_ref, blk_ref, lane_ref, act_ref, s_ref):
    xb_ref[...] = x_ref[...].astype(BF16)
    e = e_ref[...]
    blk_ref[...] = e // LANES
    lane_ref[...] = e % LANES
    act_ref[...] = jnp.zeros(act_ref.shape, F32)
    sub = U_BLOCK // LANES
    nblk = ut_ref.shape[0]

    def score(i, slot):
        s_ref[slot] = jnp.dot(xb_ref[...], ut_ref[i], preferred_element_type=F32)

    def gather(i, slot):
        lane = lane_ref[...]
        blk = blk_ref[...]
        act = act_ref[...]
        for h in range(sub):
            got = jnp.take_along_axis(s_ref[slot, :, h * LANES:(h + 1) * LANES], lane, axis=1)
            act = jnp.where(blk == i * sub + h, got, act)
        act_ref[...] = act

    score(0, 0)

    def body(i, carry):
        score(2 * i + 1, 1)
        gather(2 * i, 0)
        score(2 * i + 2, 0)
        gather(2 * i + 1, 1)
        return carry

    lax.fori_loop(0, nblk // 2 - 1, body, 0)
    score(nblk - 1, 1)
    gather(nblk - 2, 0)
    gather(nblk - 1, 1)
    coef_ref[...] = g_ref[...] * jax.nn.gelu(act_ref[...])


def _peer_u(eidx, x1, gates, ut, tb=1024):
    t, d = x1.shape
    row = lambda i: (i, 0)
    return pl.pallas_call(
        _peer_u_kernel,
        grid=(t // tb,),
        in_specs=[pl.BlockSpec((tb, d), row), pl.BlockSpec((tb, PICKS), row),
                  pl.BlockSpec((tb, PICKS), row), pl.BlockSpec(memory_space=pltpu.VMEM)],
        out_specs=pl.BlockSpec((tb, PICKS), row),
        out_shape=jax.ShapeDtypeStruct((t, PICKS), F32),
        scratch_shapes=[pltpu.VMEM((tb, d), BF16),
                        pltpu.VMEM((tb, PICKS), jnp.int32), pltpu.VMEM((tb, PICKS), jnp.int32),
                        pltpu.VMEM((tb, PICKS), F32), pltpu.VMEM((2, tb, U_BLOCK), F32)],
        compiler_params=_params(("arbitrary",)),
        name="peer_u",
    )(x1, eidx, gates, ut)


def _peer_v_kernel(e_ref, coef_ref, x_ref, g_ref, b_ref, vt_ref, y_ref, w_ref, acc_ref, *, alpha):
    tb, d = x_ref.shape
    pitch = w_ref.shape[0] // LANES
    rows = lax.broadcasted_iota(jnp.int32, (LANES, PICKS), 0)

    def scatter_group(first):
        eg = e_ref[pl.ds(first, TOK_UNROLL), :]
        cg = coef_ref[pl.ds(first, TOK_UNROLL), :]
        ig = eg // LANES
        jg = eg % LANES
        for tt in range(TOK_UNROLL):
            ib = jnp.broadcast_to(ig[tt:tt + 1, :], rows.shape)
            jb = jnp.broadcast_to(jg[tt:tt + 1, :], rows.shape)
            cb = jnp.broadcast_to(cg[tt:tt + 1, :], rows.shape)
            hot_i = jnp.where(ib == rows, 1.0, 0.0).astype(BF16)
            hot_j = jnp.where(jb == rows, cb, 0.0).astype(BF16)
            wt = lax.dot_general(hot_i, hot_j, NT_DIMS, preferred_element_type=F32)
            w_ref[pl.ds(first + tt, LANES, stride=pitch), :] = wt

    groups = 4

    def scatter(it, carry):
        for gi in range(groups):
            scatter_group(pl.multiple_of((it * groups + gi) * TOK_UNROLL, TOK_UNROLL))
        return carry

    lax.fori_loop(0, tb // (TOK_UNROLL * groups), scatter, 0)

    acc_ref[...] = jnp.zeros(acc_ref.shape, F32)
    sub = vt_ref.shape[2] // LANES

    def block(bp, carry):
        parts = [w_ref[pl.ds(pl.multiple_of((bp * sub + h) * pitch, SUBLANES), tb), :]
                 for h in range(sub)]
        wp = jnp.concatenate(parts, axis=1).astype(BF16)
        acc_ref[...] += lax.dot_general(vt_ref[bp], wp, NT_DIMS, preferred_element_type=F32)
        return carry

    lax.fori_loop(0, vt_ref.shape[0], block, 0)
    y_ref[...] = _layer_norm(alpha * x_ref[...] + acc_ref[...].T, g_ref[...], b_ref[...])


def _peer_v(eidx, coef, x1, g2, b2, vt, alpha, tb=256):
    t, d = x1.shape
    row = lambda i: (i, 0)
    vec = pl.BlockSpec((1, d), lambda i: (0, 0))
    return pl.pallas_call(
        functools.partial(_peer_v_kernel, alpha=alpha),
        grid=(t // tb,),
        in_specs=[pl.BlockSpec((tb, PICKS), row), pl.BlockSpec((tb, PICKS), row),
                  pl.BlockSpec((tb, d), row), vec, vec, pl.BlockSpec(memory_space=pltpu.VMEM)],
        out_specs=pl.BlockSpec((tb, d), row),
        out_shape=jax.ShapeDtypeStruct((t, d), F32),
        scratch_shapes=[pltpu.VMEM((LANES * (tb + SUBLANES), LANES), F32), pltpu.VMEM((d, tb), F32)],
        compiler_params=_params(("arbitrary",), vmem=PEER_V_VMEM_LIMIT),
        name="peer_v",
    )(eidx, coef, x1, g2, b2, vt)


def _trunk(x, p):
    b, s, d = x.shape
    t = b * s
    depth = p["w_in"].shape[0]
    alpha = (2 * depth) ** 0.25
    x2 = x.reshape(t, d)
    cur = x2
    gi, bi = p["ln_in_g"].reshape(1, d), p["ln_in_b"].reshape(1, d)
    for l in range(depth):
        assert l == 0, "single-layer trunk"
        linit = 0.8 - 0.6 * math.exp(-0.3 * l)
        aw = N_ATTN_HEADS * ATTN_HEAD_V
        w_in = p["w_in"][l]
        wab = _prep(w_in[:, 3 * aw:], p["w_fourier"][l])
        q, k, v, a, bm = _in_proj(cur, gi, bi, w_in[:, :3 * aw].astype(BF16), wab)
        lamv = jnp.stack([p["lambda_q1"][l], p["lambda_k1"][l], p["lambda_q2"][l], p["lambda_k2"][l]])
        hh = jnp.arange(1, N_ATTN_HEADS + 1, dtype=F32)
        slopes = jnp.exp2(-8.0 * hh / N_ATTN_HEADS)
        o = _attention(q.reshape(b, s, aw), k.reshape(b, s, aw), v.reshape(b, s, aw), lamv, slopes,
                       p["subln_g"][l].reshape(1, ATTN_HEAD_V), linit)
        f = _fourier(a.reshape(b, s, -1), bm.reshape(b, s, -1), p["b_fourier"][l].reshape(1, -1))
        x1, qp = _out_proj(o.reshape(t, aw), f.reshape(t, -1), cur, gi, bi,
                           p["w_out"][l].astype(BF16), p["ln1_g"][l].reshape(1, d),
                           p["ln1_b"][l].reshape(1, d), p["peer_wq"][l].astype(BF16), alpha)
        sk = p["peer_subkeys"][l]
        z = jnp.zeros_like(sk)
        sk_ext = jnp.stack([jnp.concatenate([sk[:, 0], z[:, 0]], axis=-1),
                            jnp.concatenate([z[:, 1], sk[:, 1]], axis=-1)], axis=1).astype(BF16)
        e_t, g_t = _topk(qp, sk_ext)
        eidx = e_t.reshape(PICKS, t).T
        gates = g_t.reshape(PICKS, t).T
        coef = _peer_u(eidx, x1, gates, _block_transposed(p["peer_u"][l], U_BLOCK))
        cur = _peer_v(eidx, coef, x1, p["ln2_g"][l].reshape(1, d), p["ln2_b"][l].reshape(1, d),
                      _block_transposed(p["peer_v"][l], V_BLOCK), alpha)
    return cur.reshape(b, s, d)


def kernel(x_prompt, x_sample, ln_in_g, ln_in_b, w_in, lambda_q1, lambda_k1, lambda_q2, lambda_k2,
           subln_g, w_fourier, b_fourier, w_out, ln1_g, ln1_b, peer_wq, peer_subkeys, peer_u, peer_v,
           ln2_g, ln2_b):
    p = dict(ln_in_g=ln_in_g, ln_in_b=ln_in_b, w_in=w_in, lambda_q1=lambda_q1, lambda_k1=lambda_k1,
             lambda_q2=lambda_q2, lambda_k2=lambda_k2, subln_g=subln_g, w_fourier=w_fourier,
             b_fourier=b_fourier, w_out=w_out, ln1_g=ln1_g, ln1_b=ln1_b, peer_wq=peer_wq,
             peer_subkeys=peer_subkeys, peer_u=peer_u, peer_v=peer_v, ln2_g=ln2_g, ln2_b=ln2_b)
    return (_trunk(x_prompt, p), _trunk(x_sample, p))
```

```python
import functools
import math

import jax
import jax.numpy as jnp
from jax import lax
from jax.experimental import pallas as pl
from jax.experimental.pallas import tpu as pltpu

F32 = jnp.float32
BF16 = jnp.bfloat16

LN_EPS = 1e-5
RMS_EPS = 1e-5
N_ATTN_HEADS = 4
ATTN_HEAD_V = 128
ATTN_HEAD_QK = 64
N_FOURIER_GROUPS = 4
FOURIER_GROUP = 128
PEER_HEADS = 8
PEER_N_KEYS = 128
PEER_HALF = 64
PEER_TOPK = 16
PICKS = PEER_HEADS * PEER_TOPK

LANES = 128
SUBLANES = 8
VMEM_LIMIT = 56 * 1024 * 1024
BIG_VMEM_LIMIT = 60 * 1024 * 1024

NT_DIMS = (((1,), (1,)), ((), ()))


def _params(sem, vmem=VMEM_LIMIT):
    return pltpu.CompilerParams(dimension_semantics=sem, vmem_limit_bytes=vmem)


def _layer_norm(x, g, b):
    mu = jnp.mean(x, axis=-1, keepdims=True)
    xc = x - mu
    var = jnp.mean(xc * xc, axis=-1, keepdims=True)
    return xc * lax.rsqrt(var + LN_EPS) * g + b


def _prep_kernel(win_f_ref, wf_ref, c_ref, s_ref, wab_ref):
    hp = lax.Precision.HIGHEST
    for g in range(N_FOURIER_GROUPS):
        wg = wf_ref[g]
        cw = jnp.dot(c_ref[...], wg, precision=hp, preferred_element_type=F32)
        sw = jnp.dot(s_ref[...], wg, precision=hp, preferred_element_type=F32)
        wi = win_f_ref[:, g * FOURIER_GROUP:(g + 1) * FOURIER_GROUP]
        a = jnp.dot(wi, cw, precision=hp, preferred_element_type=F32)
        b = jnp.dot(wi, sw, precision=hp, preferred_element_type=F32)
        wab_ref[:, g * FOURIER_GROUP:(g + 1) * FOURIER_GROUP] = a.astype(BF16)
        wab_ref[:, (N_FOURIER_GROUPS + g) * FOURIER_GROUP:
                (N_FOURIER_GROUPS + g + 1) * FOURIER_GROUP] = b.astype(BF16)


def _prep(win_f, w_fourier):
    d = win_f.shape[0]
    n = FOURIER_GROUP
    idx = jnp.arange(n, dtype=jnp.int32)
    ang = ((idx[:, None] * idx[None, :]) % n).astype(F32) * (2.0 * math.pi / n)
    cmat = jnp.cos(ang) * (n ** -0.5)
    smat = jnp.sin(ang) * (n ** -0.5)
    return pl.pallas_call(
        _prep_kernel,
        out_shape=jax.ShapeDtypeStruct((d, 2 * N_FOURIER_GROUPS * n), BF16),
        name="prep",
    )(win_f, w_fourier, cmat, smat)


def _inproj_kernel(x_ref, g_ref, b_ref, wqkv_ref, wab_ref, q_ref, k_ref, v_ref, a_ref, bb_ref):
    xn = _layer_norm(x_ref[...], g_ref[...], b_ref[...]).astype(BF16)
    h = jnp.dot(xn, wqkv_ref[...], preferred_element_type=F32)
    aw = q_ref.shape[1]
    q_ref[...] = (h[:, :aw] * (ATTN_HEAD_QK ** -0.5)).astype(BF16)
    k_ref[...] = h[:, aw:2 * aw].astype(BF16)
    v_ref[...] = h[:, 2 * aw:3 * aw].astype(BF16)
    ab = jnp.dot(xn, wab_ref[...], preferred_element_type=F32)
    fw = a_ref.shape[1]
    a_ref[...] = ab[:, :fw].astype(BF16)
    bb_ref[...] = ab[:, fw:].astype(BF16)


def _in_proj(x2, g, b, wqkv, wab, tm=512):
    t, d = x2.shape
    aw = wqkv.shape[1] // 3
    fw = wab.shape[1] // 2
    row = lambda i: (i, 0)
    full = lambda i: (0, 0)
    outs = [jax.ShapeDtypeStruct((t, aw), BF16)] * 3 + [jax.ShapeDtypeStruct((t, fw), BF16)] * 2
    return pl.pallas_call(
        _inproj_kernel,
        grid=(t // tm,),
        in_specs=[pl.BlockSpec((tm, d), row), pl.BlockSpec((1, d), full), pl.BlockSpec((1, d), full),
                  pl.BlockSpec(wqkv.shape, full), pl.BlockSpec(wab.shape, full)],
        out_specs=[pl.BlockSpec((tm, aw), row)] * 3 + [pl.BlockSpec((tm, fw), row)] * 2,
        out_shape=outs,
        compiler_params=_params(("parallel",)),
        name="in_proj",
    )(x2, g, b, wqkv, wab)


def _attn_kernel(lamv_ref, slope_ref, q_ref, k_ref, v_ref, g_ref, o_ref,
                 ka_ref, s_ref, m_ref, l_ref, acc_ref, *, tq, tk, linit):
    h = pl.program_id(1)
    i = pl.program_id(2)
    s_len = k_ref.shape[0]
    nkv = s_len // tk
    nl = tk // LANES
    slope = slope_ref[h]
    lv = lamv_ref[...]
    lam = (jnp.exp(jnp.sum(lv[0:1] * lv[1:2], axis=-1, keepdims=True))
           - jnp.exp(jnp.sum(lv[2:3] * lv[3:4], axis=-1, keepdims=True)) + linit)
    dq = ATTN_HEAD_QK

    def aug_lanes(shape, first_pos, c):
        lane = lax.broadcasted_iota(jnp.int32, shape, 1)
        pos = first_pos + lax.broadcasted_iota(jnp.int32, shape, 0)
        hi = (pos // dq).astype(F32) * (slope * dq)
        lo = (pos % dq).astype(F32) * slope
        base = dq * (1 - c)
        keep = (lane < dq) if c == 0 else (lane >= dq)
        return lane - base, hi, lo, keep

    @pl.when(i == 0)
    def _():
        k = k_ref[...]
        for c in range(2):
            rel, hi, lo, keep = aug_lanes(k.shape, 0, c)
            aug = jnp.where(rel == 0, hi, jnp.where(rel == 1, lo,
                            jnp.where((rel == 2) | (rel == 3), 1.0, 0.0)))
            ka_ref[c] = jnp.where(keep, k, aug.astype(BF16))

    q = q_ref[...]
    q_left, q_right, q_diag = [], [], []
    for c in range(2):
        rel, hi, lo, keep = aug_lanes(q.shape, i * tq, c)
        aug = jnp.where((rel == 0) | (rel == 1), 1.0,
                        jnp.where(rel == 2, -hi, jnp.where(rel == 3, -lo, 0.0)))
        q_left.append(jnp.where(keep, q, aug.astype(BF16)))
        q_right.append(jnp.where(keep, q, (-aug).astype(BF16)))
        q_diag.append(jnp.where(keep, q, jnp.zeros_like(q)))

    m_ref[...] = jnp.full(m_ref.shape, -jnp.inf, F32)

    def scores(j, qv, bias):
        start = pl.multiple_of(j * tk, tk)
        for c in range(2):
            s = lax.dot_general(qv[c], ka_ref[c, pl.ds(start, tk), :], NT_DIMS,
                                preferred_element_type=F32)
            if bias is not None:
                s = s - bias
            s_ref[c, j] = s
            mt = m_ref[c]
            for l in range(nl):
                mt = jnp.maximum(mt, s[:, l * LANES:(l + 1) * LANES])
            m_ref[c] = mt

    jd = (i * tq) // tk
    lax.fori_loop(0, jd, lambda j, cr: (scores(j, q_left, None), cr)[1], 0)
    qpos = (i * tq + lax.broadcasted_iota(jnp.int32, (tq, 1), 0)).astype(F32)
    kpos = (jd * tk + lax.broadcasted_iota(jnp.int32, (1, tk), 1)).astype(F32)
    scores(jd, q_diag, slope * jnp.abs(qpos - kpos))
    lax.fori_loop(jd + 1, nkv, lambda j, cr: (scores(j, q_right, None), cr)[1], 0)

    for c in range(2):
        m_ref[c] = jnp.broadcast_to(jnp.max(m_ref[c], axis=-1, keepdims=True), (tq, LANES))
    l_ref[...] = jnp.zeros(l_ref.shape, F32)
    acc_ref[...] = jnp.zeros(acc_ref.shape, F32)

    def accumulate(j, carry):
        vb = v_ref[pl.ds(pl.multiple_of(j * tk, tk), tk), :]
        for c in range(2):
            s = s_ref[c, j]
            mb = m_ref[c]
            ps = [jnp.exp(s[:, l * LANES:(l + 1) * LANES] - mb) for l in range(nl)]
            lsum = l_ref[c]
            for p in ps:
                lsum = lsum + p
            l_ref[c] = lsum
            acc_ref[c] += jnp.dot(jnp.concatenate(ps, axis=1).astype(BF16), vb,
                                  preferred_element_type=F32)
        return carry

    lax.fori_loop(0, nkv, accumulate, 0)
    l0 = jnp.sum(l_ref[0], axis=-1, keepdims=True)
    l1 = jnp.sum(l_ref[1], axis=-1, keepdims=True)
    o = acc_ref[0] / l0 - lam * (acc_ref[1] / l1)
    y = o * lax.rsqrt(jnp.mean(o * o, axis=-1, keepdims=True) + RMS_EPS) * g_ref[...]
    o_ref[...] = (y * (1.0 - linit)).astype(o_ref.dtype)


ATTN_SCORE_BYTES = 32 * 1024 * 1024


def _attention(q, k, v, lamv, slopes, subln_g, linit, tk=1024):
    b, s, aw = q.shape
    hv = ATTN_HEAD_V
    tq = min(512, ATTN_SCORE_BYTES // (2 * 4 * s))
    assert tk % tq == 0 and s % tk == 0
    qspec = pl.BlockSpec((None, tq, hv), lambda bi, hi, qi: (bi, qi, hi))
    kvspec = pl.BlockSpec((None, s, hv), lambda bi, hi, qi: (bi, 0, hi))
    return pl.pallas_call(
        functools.partial(_attn_kernel, tq=tq, tk=tk, linit=linit),
        grid=(b, aw // hv, s // tq),
        in_specs=[pl.BlockSpec((4, ATTN_HEAD_QK), lambda bi, hi, qi: (0, 0)),
                  pl.BlockSpec(memory_space=pltpu.SMEM),
                  qspec, kvspec, kvspec,
                  pl.BlockSpec((1, hv), lambda bi, hi, qi: (0, 0))],
        out_specs=qspec,
        out_shape=jax.ShapeDtypeStruct((b, s, aw), BF16),
        scratch_shapes=[pltpu.VMEM((2, s, hv), BF16), pltpu.VMEM((2, s // tk, tq, tk), F32),
                        pltpu.VMEM((2, tq, LANES), F32), pltpu.VMEM((2, tq, LANES), F32),
                        pltpu.VMEM((2, tq, hv), F32)],
        compiler_params=_params(("parallel", "parallel", "arbitrary"), vmem=BIG_VMEM_LIMIT),
        name="attn",
    )(lamv, slopes, q, k, v, subln_g)


DFT_FINE = 64


def _fourier_kernel(ca_ref, nsa_ref, cr_ref, nsr_ref, a_ref, b_ref, bias_ref, f_ref, acc_ref):
    kk = pl.program_id(2)

    @pl.when(kk == 0)
    def _():
        acc_ref[...] = jnp.zeros(acc_ref.shape, F32)

    cr = cr_ref[...]
    nsr = nsr_ref[...]
    cos_rows, nsin_rows = [], []
    for a in range(ca_ref.shape[0]):
        ca = ca_ref[a:a + 1, :]
        nsa = nsa_ref[a:a + 1, :]
        cos_rows.append((ca * cr - nsa * nsr).astype(BF16))
        nsin_rows.append((nsa * cr + ca * nsr).astype(BF16))
    acc_ref[...] += (jnp.dot(jnp.concatenate(cos_rows, axis=0), a_ref[...], preferred_element_type=F32)
                     + jnp.dot(jnp.concatenate(nsin_rows, axis=0), b_ref[...],
                               preferred_element_type=F32))

    @pl.when(kk == pl.num_programs(2) - 1)
    def _():
        f_ref[...] = (acc_ref[...] + bias_ref[...]).astype(f_ref.dtype)


def _dft_tables(s):
    k = jnp.arange(s, dtype=jnp.int32)[None, :]

    def cos_nsin(rows, scale):
        ang = ((rows[:, None] * k) % s).astype(F32) * (2.0 * math.pi / s)
        return jnp.cos(ang) * scale, jnp.sin(ang) * (-scale)

    coarse = cos_nsin(jnp.arange(s // DFT_FINE, dtype=jnp.int32) * DFT_FINE, s ** -0.5)
    fine = cos_nsin(jnp.arange(DFT_FINE, dtype=jnp.int32), 1.0)
    return coarse + fine


def _fourier(a, bm, bias, tm=1024, tk=1024):
    b, s, fw = a.shape
    ca, nsa, cr, nsr = _dft_tables(s)
    cspec = pl.BlockSpec((tm // DFT_FINE, tk), lambda bi, i, kk: (i, kk))
    fspec = pl.BlockSpec((DFT_FINE, tk), lambda bi, i, kk: (0, kk))
    xspec = pl.BlockSpec((None, tk, fw), lambda bi, i, kk: (bi, kk, 0))
    return pl.pallas_call(
        _fourier_kernel,
        grid=(b, s // tm, s // tk),
        in_specs=[cspec, cspec, fspec, fspec, xspec, xspec,
                  pl.BlockSpec((1, fw), lambda bi, i, kk: (0, 0))],
        out_specs=pl.BlockSpec((None, tm, fw), lambda bi, i, kk: (bi, i, 0)),
        out_shape=jax.ShapeDtypeStruct((b, s, fw), BF16),
        scratch_shapes=[pltpu.VMEM((tm, fw), F32)],
        compiler_params=_params(("parallel", "parallel", "arbitrary")),
        name="fourier",
    )(ca, nsa, cr, nsr, a, bm, bias)


def _outproj_kernel(o_ref, f_ref, x_ref, gi_ref, bi_ref, wo_ref, g1_ref, b1_ref, wq_ref,
                    x1_ref, qp_ref, *, alpha):
    aw = o_ref.shape[1]
    mix = (jnp.dot(o_ref[...], wo_ref[:aw, :], preferred_element_type=F32)
           + jnp.dot(f_ref[...], wo_ref[aw:, :], preferred_element_type=F32))
    xn = _layer_norm(x_ref[...], gi_ref[...], bi_ref[...])
    x1 = _layer_norm(alpha * xn + mix, g1_ref[...], b1_ref[...])
    x1_ref[...] = x1
    qp_ref[...] = jnp.dot(x1.astype(BF16), wq_ref[...], preferred_element_type=F32)


def _out_proj(o2, f2, x2, gi, bi, wo, g1, b1, wq, alpha, tm=512):
    t, d = x2.shape
    row = lambda i: (i, 0)
    full = lambda i: (0, 0)
    vec = pl.BlockSpec((1, d), full)
    return pl.pallas_call(
        functools.partial(_outproj_kernel, alpha=alpha),
        grid=(t // tm,),
        in_specs=[pl.BlockSpec((tm, o2.shape[1]), row), pl.BlockSpec((tm, f2.shape[1]), row),
                  pl.BlockSpec((tm, d), row), vec, vec, pl.BlockSpec(wo.shape, full), vec, vec,
                  pl.BlockSpec(wq.shape, full)],
        out_specs=[pl.BlockSpec((tm, d), row), pl.BlockSpec((tm, wq.shape[1]), row)],
        out_shape=[jax.ShapeDtypeStruct((t, d), F32), jax.ShapeDtypeStruct((t, wq.shape[1]), F32)],
        compiler_params=_params(("parallel",)),
        name="out_proj",
    )(o2, f2, x2, gi, bi, wo, g1, b1, wq)


def _topk_cols(s, pos, k, payload=None):
    return _topk_cols_multi([s], pos, k, payload)[0]


def _topk_cols_multi(scores, pos, k, payload=None):
    big = jnp.float32(1e9)
    scores = list(scores)
    outs = [([], []) for _ in scores]
    for _ in range(k):
        for n, s in enumerate(scores):
            m = jnp.max(s, axis=0, keepdims=True)
            am = jnp.min(jnp.where(s == m, pos, big), axis=0, keepdims=True)
            hit = pos == am
            outs[n][0].append(m)
            if payload is None:
                outs[n][1].append(am)
            else:
                outs[n][1].append(jnp.max(jnp.where(hit, payload, -1.0), axis=0, keepdims=True))
            scores[n] = jnp.where(hit, -jnp.inf, s)
    return outs


def _pair_candidates(s1r, i1r, s2r, i2r):
    kk = PEER_TOPK
    tb = s1r[0].shape[1]
    r16 = lax.broadcasted_iota(jnp.int32, (kk, tb), 0).astype(F32)
    r8 = lax.broadcasted_iota(jnp.int32, (SUBLANES, tb), 0).astype(F32)
    nk = float(PEER_N_KEYS)
    ninf = -jnp.inf
    cat = lambda rows: jnp.concatenate(rows, axis=0)
    s1, i1, s2, i2 = cat(s1r), cat(i1r), cat(s2r), cat(i2r)
    s1l, i1l, s2l, i2l = (cat(r[:SUBLANES]) for r in (s1r, i1r, s2r, i2r))
    groups = []
    groups.append((s1 + s2r[0], kk * r16, i1 * nk + i2r[0], None))
    groups.append((s1r[0] + s2, r16, i1r[0] * nk + i2, r16 >= 1.0))
    groups.append((s1l + s2r[1], kk * r8 + 1.0, i1l * nk + i2r[1], r8 >= 1.0))
    groups.append((s1r[1] + s2l, kk + r8, i1r[1] * nk + i2l, r8 >= 2.0))
    for b in (2, 3, 4):
        amax = kk // (b + 1) - 1
        groups.append((s1l + s2r[b], kk * r8 + float(b), i1l * nk + i2r[b],
                       (r8 >= 2.0) & (r8 <= float(amax))))
    val = jnp.concatenate([v if ok is None else jnp.where(ok, v, ninf) for v, _, _, ok in groups], axis=0)
    pos = jnp.concatenate([p for _, p, _, _ in groups], axis=0)
    eid = jnp.concatenate([e for _, _, e, _ in groups], axis=0)
    return val, pos, eid


def _topk_kernel(q_ref, sk_ref, e_ref, g_ref):
    q = q_ref[...].astype(BF16)
    tb = q.shape[0]
    key_pos = lax.broadcasted_iota(jnp.int32, (PEER_N_KEYS, tb), 0).astype(F32)
    sc = [lax.dot_general(sk_ref[c], q, NT_DIMS, preferred_element_type=F32)
          for c in range(2)]
    (s1, i1), (s2, i2) = _topk_cols_multi(sc, key_pos, PEER_TOPK)
    val, pos, eid = _pair_candidates(s1, i1, s2, i2)
    bestr, eselr = _topk_cols(val, pos, PEER_TOPK, payload=eid)
    best = jnp.concatenate(bestr, axis=0)
    ex = jnp.exp(best - bestr[0])
    e_ref[...] = jnp.concatenate(eselr, axis=0).astype(jnp.int32)
    g_ref[...] = ex / jnp.sum(ex, axis=0, keepdims=True)


def _topk(qp, sk_ext, tb=256):
    t = qp.shape[0]
    hp = PEER_HEADS
    ospec = pl.BlockSpec((None, PEER_TOPK, tb), lambda i, h: (h, 0, i))
    return pl.pallas_call(
        _topk_kernel,
        grid=(t // tb, hp),
        in_specs=[pl.BlockSpec((tb, LANES), lambda i, h: (i, h)),
                  pl.BlockSpec((None, 2, PEER_N_KEYS, LANES), lambda i, h: (h, 0, 0, 0))],
        out_specs=[ospec, ospec],
        out_shape=[jax.ShapeDtypeStruct((hp, PEER_TOPK, t), jnp.int32),
                   jax.ShapeDtypeStruct((hp, PEER_TOPK, t), F32)],
        compiler_params=_params(("parallel", "parallel")),
        name="topk",
    )(qp, sk_ext)


TOK_UNROLL = 8
U_BLOCK = 256


V_BLOCK = 1024


def _block_transposed(table, block):
    n, d = table.shape
    return table.astype(BF16).reshape(n // block, block, d).transpose(0, 2, 1)


def _peer_u_kernel(x_ref, e_ref, g_ref, ut_ref, coef_ref, xb_ref, blk_ref, lane_ref, act_ref, s_ref):
    xb_ref[...] = x_ref[...].astype(BF16)
    e = e_ref[...]
    blk_ref[...] = e // LANES
    lane_ref[...] = e % LANES
    act_ref[...] = jnp.zeros(act_ref.shape, F32)
    sub = U_BLOCK // LANES
    nblk = ut_ref.shape[0]

    def score(i, slot):
        s_ref[slot] = jnp.dot(xb_ref[...], ut_ref[i], preferred_element_type=F32)

    def gather(i, slot):
        lane = lane_ref[...]
        blk = blk_ref[...]
        act = act_ref[...]
        for h in range(sub):
            got = jnp.take_along_axis(s_ref[slot, :, h * LANES:(h + 1) * LANES], lane, axis=1)
            act = jnp.where(blk == i * sub + h, got, act)
        act_ref[...] = act

    nslot = s_ref.shape[0]
    score(0, 0)

    def body(i, carry):
        for r in range(nslot):
            score(nslot * i + r + 1, (r + 1) % nslot)
            gather(nslot * i + r, r)
        return carry

    lax.fori_loop(0, nblk // nslot - 1, body, 0)
    for r in range(nslot):
        blk_i = nblk - nslot + r
        if r + 1 < nslot:
            score(blk_i + 1, r + 1)
        gather(blk_i, r)
    coef_ref[...] = g_ref[...] * jax.nn.gelu(act_ref[...])


def _peer_u(eidx, x1, gates, ut, tb=1024):
    t, d = x1.shape
    row = lambda i: (i, 0)
    return pl.pallas_call(
        _peer_u_kernel,
        grid=(t // tb,),
        in_specs=[pl.BlockSpec((tb, d), row), pl.BlockSpec((tb, PICKS), row),
                  pl.BlockSpec((tb, PICKS), row), pl.BlockSpec(memory_space=pltpu.VMEM)],
        out_specs=pl.BlockSpec((tb, PICKS), row),
        out_shape=jax.ShapeDtypeStruct((t, PICKS), F32),
        scratch_shapes=[pltpu.VMEM((tb, d), BF16),
                        pltpu.VMEM((tb, PICKS), jnp.int32), pltpu.VMEM((tb, PICKS), jnp.int32),
                        pltpu.VMEM((tb, PICKS), F32), pltpu.VMEM((4, tb, U_BLOCK), F32)],
        compiler_params=_params(("arbitrary",)),
        name="peer_u",
    )(x1, eidx, gates, ut)


def _peer_v_kernel(e_ref, coef_ref, x_ref, g_ref, b_ref, vt_ref, y_ref, w_ref, acc_ref, *, alpha):
    tb, d = x_ref.shape
    pitch = w_ref.shape[0] // LANES
    rows = lax.broadcasted_iota(jnp.int32, (LANES, PICKS), 0)

    def scatter_group(first):
        eg = e_ref[pl.ds(first, TOK_UNROLL), :]
        cg = coef_ref[pl.ds(first, TOK_UNROLL), :]
        ig = eg // LANES
        jg = eg % LANES
        for tt in range(TOK_UNROLL):
            ib = jnp.broadcast_to(ig[tt:tt + 1, :], rows.shape)
            jb = jnp.broadcast_to(jg[tt:tt + 1, :], rows.shape)
            cb = jnp.broadcast_to(cg[tt:tt + 1, :], rows.shape)
            hot_i = jnp.where(ib == rows, 1.0, 0.0).astype(BF16)
            hot_j = jnp.where(jb == rows, cb, 0.0).astype(BF16)
            wt = lax.dot_general(hot_i, hot_j, NT_DIMS, preferred_element_type=F32)
            w_ref[pl.ds(first + tt, LANES, stride=pitch), :] = wt

    groups = 4

    def scatter(it, carry):
        for gi in range(groups):
            scatter_group(pl.multiple_of((it * groups + gi) * TOK_UNROLL, TOK_UNROLL))
        return carry

    lax.fori_loop(0, tb // (TOK_UNROLL * groups), scatter, 0)

    acc_ref[...] = jnp.zeros(acc_ref.shape, F32)
    sub = vt_ref.shape[2] // LANES

    def block(bp, carry):
        parts = [w_ref[pl.ds(pl.multiple_of((bp * sub + h) * pitch, SUBLANES), tb), :]
                 for h in range(sub)]
        wp = jnp.concatenate(parts, axis=1).astype(BF16)
        acc_ref[...] += lax.dot_general(vt_ref[bp], wp, NT_DIMS, preferred_element_type=F32)
        return carry

    lax.fori_loop(0, vt_ref.shape[0], block, 0)
    y_ref[...] = _layer_norm(alpha * x_ref[...] + acc_ref[...].T, g_ref[...], b_ref[...])


def _peer_v(eidx, coef, x1, g2, b2, vt, alpha, tb=256):
    t, d = x1.shape
    row = lambda i: (i, 0)
    vec = pl.BlockSpec((1, d), lambda i: (0, 0))
    return pl.pallas_call(
        functools.partial(_peer_v_kernel, alpha=alpha),
        grid=(t // tb,),
        in_specs=[pl.BlockSpec((tb, PICKS), row), pl.BlockSpec((tb, PICKS), row),
                  pl.BlockSpec((tb, d), row), vec, vec, pl.BlockSpec(memory_space=pltpu.VMEM)],
        out_specs=pl.BlockSpec((tb, d), row),
        out_shape=jax.ShapeDtypeStruct((t, d), F32),
        scratch_shapes=[pltpu.VMEM((LANES * (tb + SUBLANES), LANES), F32), pltpu.VMEM((d, tb), F32)],
        compiler_params=_params(("arbitrary",), vmem=BIG_VMEM_LIMIT),
        name="peer_v",
    )(eidx, coef, x1, g2, b2, vt)


def _trunk(x, p):
    b, s, d = x.shape
    t = b * s
    depth = p["w_in"].shape[0]
    alpha = (2 * depth) ** 0.25
    x2 = x.reshape(t, d)
    cur = x2
    gi, bi = p["ln_in_g"].reshape(1, d), p["ln_in_b"].reshape(1, d)
    for l in range(depth):
        assert l == 0, "single-layer trunk"
        linit = 0.8 - 0.6 * math.exp(-0.3 * l)
        aw = N_ATTN_HEADS * ATTN_HEAD_V
        w_in = p["w_in"][l]
        wab = _prep(w_in[:, 3 * aw:], p["w_fourier"][l])
        q, k, v, a, bm = _in_proj(cur, gi, bi, w_in[:, :3 * aw].astype(BF16), wab)
        lamv = jnp.stack([p["lambda_q1"][l], p["lambda_k1"][l], p["lambda_q2"][l], p["lambda_k2"][l]])
        hh = jnp.arange(1, N_ATTN_HEADS + 1, dtype=F32)
        slopes = jnp.exp2(-8.0 * hh / N_ATTN_HEADS)
        o = _attention(q.reshape(b, s, aw), k.reshape(b, s, aw), v.reshape(b, s, aw), lamv, slopes,
                       p["subln_g"][l].reshape(1, ATTN_HEAD_V), linit)
        f = _fourier(a.reshape(b, s, -1), bm.reshape(b, s, -1), p["b_fourier"][l].reshape(1, -1))
        x1, qp = _out_proj(o.reshape(t, aw), f.reshape(t, -1), cur, gi, bi,
                           p["w_out"][l].astype(BF16), p["ln1_g"][l].reshape(1, d),
                           p["ln1_b"][l].reshape(1, d), p["peer_wq"][l].astype(BF16), alpha)
        sk = p["peer_subkeys"][l]
        z = jnp.zeros_like(sk)
        sk_ext = jnp.stack([jnp.concatenate([sk[:, 0], z[:, 0]], axis=-1),
                            jnp.concatenate([z[:, 1], sk[:, 1]], axis=-1)], axis=1).astype(BF16)
        e_t, g_t = _topk(qp, sk_ext)
        eidx = e_t.reshape(PICKS, t).T
        gates = g_t.reshape(PICKS, t).T
        coef = _peer_u(eidx, x1, gates, _block_transposed(p["peer_u"][l], U_BLOCK))
        cur = _peer_v(eidx, coef, x1, p["ln2_g"][l].reshape(1, d), p["ln2_b"][l].reshape(1, d),
                      _block_transposed(p["peer_v"][l], V_BLOCK), alpha)
    return cur.reshape(b, s, d)


def kernel(x_prompt, x_sample, ln_in_g, ln_in_b, w_in, lambda_q1, lambda_k1, lambda_q2, lambda_k2,
           subln_g, w_fourier, b_fourier, w_out, ln1_g, ln1_b, peer_wq, peer_subkeys, peer_u, peer_v,
           ln2_g, ln2_b):
    p = dict(ln_in_g=ln_in_g, ln_in_b=ln_in_b, w_in=w_in, lambda_q1=lambda_q1, lambda_k1=lambda_k1,
             lambda_q2=lambda_q2, lambda_k2=lambda_k2, subln_g=subln_g, w_fourier=w_fourier,
             b_fourier=b_fourier, w_out=w_out, ln1_g=ln1_g, ln1_b=ln1_b, peer_wq=peer_wq,
             peer_subkeys=peer_subkeys, peer_u=peer_u, peer_v=peer_v, ln2_g=ln2_g, ln2_b=ln2_b)
    return (_trunk(x_prompt, p), _trunk(x_sample, p))
```

```python
import functools
import math

import jax
import jax.numpy as jnp
from jax import lax
from jax.experimental import pallas as pl
from jax.experimental.pallas import tpu as pltpu

F32 = jnp.float32
BF16 = jnp.bfloat16

LN_EPS = 1e-5
RMS_EPS = 1e-5
N_ATTN_HEADS = 4
ATTN_HEAD_V = 128
ATTN_HEAD_QK = 64
N_FOURIER_GROUPS = 4
FOURIER_GROUP = 128
PEER_HEADS = 8
PEER_N_KEYS = 128
PEER_HALF = 64
PEER_TOPK = 16
PICKS = PEER_HEADS * PEER_TOPK

LANES = 128
SUBLANES = 8
VMEM_LIMIT = 56 * 1024 * 1024
BIG_VMEM_LIMIT = 60 * 1024 * 1024

NT_DIMS = (((1,), (1,)), ((), ()))


def _params(sem, vmem=VMEM_LIMIT):
    return pltpu.CompilerParams(dimension_semantics=sem, vmem_limit_bytes=vmem)


def _layer_norm(x, g, b):
    mu = jnp.mean(x, axis=-1, keepdims=True)
    xc = x - mu
    var = jnp.mean(xc * xc, axis=-1, keepdims=True)
    return xc * lax.rsqrt(var + LN_EPS) * g + b


def _prep_kernel(win_f_ref, wf_ref, c_ref, s_ref, wab_ref):
    hp = lax.Precision.HIGHEST
    for g in range(N_FOURIER_GROUPS):
        wg = wf_ref[g]
        cw = jnp.dot(c_ref[...], wg, precision=hp, preferred_element_type=F32)
        sw = jnp.dot(s_ref[...], wg, precision=hp, preferred_element_type=F32)
        wi = win_f_ref[:, g * FOURIER_GROUP:(g + 1) * FOURIER_GROUP]
        a = jnp.dot(wi, cw, precision=hp, preferred_element_type=F32)
        b = jnp.dot(wi, sw, precision=hp, preferred_element_type=F32)
        wab_ref[:, g * FOURIER_GROUP:(g + 1) * FOURIER_GROUP] = a.astype(BF16)
        wab_ref[:, (N_FOURIER_GROUPS + g) * FOURIER_GROUP:
                (N_FOURIER_GROUPS + g + 1) * FOURIER_GROUP] = b.astype(BF16)


def _prep(win_f, w_fourier):
    d = win_f.shape[0]
    n = FOURIER_GROUP
    idx = jnp.arange(n, dtype=jnp.int32)
    ang = ((idx[:, None] * idx[None, :]) % n).astype(F32) * (2.0 * math.pi / n)
    cmat = jnp.cos(ang) * (n ** -0.5)
    smat = jnp.sin(ang) * (n ** -0.5)
    return pl.pallas_call(
        _prep_kernel,
        out_shape=jax.ShapeDtypeStruct((d, 2 * N_FOURIER_GROUPS * n), BF16),
        name="prep",
    )(win_f, w_fourier, cmat, smat)


def _inproj_kernel(x_ref, g_ref, b_ref, wqkv_ref, wab_ref, q_ref, k_ref, v_ref, a_ref, bb_ref):
    xn = _layer_norm(x_ref[...], g_ref[...], b_ref[...]).astype(BF16)
    h = jnp.dot(xn, wqkv_ref[...], preferred_element_type=F32)
    aw = q_ref.shape[1]
    q_ref[...] = (h[:, :aw] * (ATTN_HEAD_QK ** -0.5)).astype(BF16)
    k_ref[...] = h[:, aw:2 * aw].astype(BF16)
    v_ref[...] = h[:, 2 * aw:3 * aw].astype(BF16)
    ab = jnp.dot(xn, wab_ref[...], preferred_element_type=F32)
    fw = a_ref.shape[1]
    a_ref[...] = ab[:, :fw].astype(BF16)
    bb_ref[...] = ab[:, fw:].astype(BF16)


def _in_proj(x2, g, b, wqkv, wab, tm=512):
    t, d = x2.shape
    aw = wqkv.shape[1] // 3
    fw = wab.shape[1] // 2
    row = lambda i: (i, 0)
    full = lambda i: (0, 0)
    outs = [jax.ShapeDtypeStruct((t, aw), BF16)] * 3 + [jax.ShapeDtypeStruct((t, fw), BF16)] * 2
    return pl.pallas_call(
        _inproj_kernel,
        grid=(t // tm,),
        in_specs=[pl.BlockSpec((tm, d), row), pl.BlockSpec((1, d), full), pl.BlockSpec((1, d), full),
                  pl.BlockSpec(wqkv.shape, full), pl.BlockSpec(wab.shape, full)],
        out_specs=[pl.BlockSpec((tm, aw), row)] * 3 + [pl.BlockSpec((tm, fw), row)] * 2,
        out_shape=outs,
        compiler_params=_params(("parallel",)),
        name="in_proj",
    )(x2, g, b, wqkv, wab)


def _attn_kernel(lamv_ref, slope_ref, q_ref, k_ref, v_ref, g_ref, o_ref,
                 ka_ref, s_ref, m_ref, l_ref, acc_ref, *, tq, tk, linit):
    h = pl.program_id(1)
    i = pl.program_id(2)
    s_len = k_ref.shape[0]
    nkv = s_len // tk
    nl = tk // LANES
    slope = slope_ref[h]
    lv = lamv_ref[...]
    lam = (jnp.exp(jnp.sum(lv[0:1] * lv[1:2], axis=-1, keepdims=True))
           - jnp.exp(jnp.sum(lv[2:3] * lv[3:4], axis=-1, keepdims=True)) + linit)
    dq = ATTN_HEAD_QK

    def aug_lanes(shape, first_pos, c):
        lane = lax.broadcasted_iota(jnp.int32, shape, 1)
        pos = first_pos + lax.broadcasted_iota(jnp.int32, shape, 0)
        hi = (pos // dq).astype(F32) * (slope * dq)
        lo = (pos % dq).astype(F32) * slope
        base = dq * (1 - c)
        keep = (lane < dq) if c == 0 else (lane >= dq)
        return lane - base, hi, lo, keep

    @pl.when(i == 0)
    def _():
        k = k_ref[...]
        for c in range(2):
            rel, hi, lo, keep = aug_lanes(k.shape, 0, c)
            aug = jnp.where(rel == 0, hi, jnp.where(rel == 1, lo,
                            jnp.where((rel == 2) | (rel == 3), 1.0, 0.0)))
            ka_ref[c] = jnp.where(keep, k, aug.astype(BF16))

    q = q_ref[...]
    q_left, q_right, q_diag = [], [], []
    for c in range(2):
        rel, hi, lo, keep = aug_lanes(q.shape, i * tq, c)
        aug = jnp.where((rel == 0) | (rel == 1), 1.0,
                        jnp.where(rel == 2, -hi, jnp.where(rel == 3, -lo, 0.0)))
        q_left.append(jnp.where(keep, q, aug.astype(BF16)))
        q_right.append(jnp.where(keep, q, (-aug).astype(BF16)))
        q_diag.append(jnp.where(keep, q, jnp.zeros_like(q)))

    m_ref[...] = jnp.full(m_ref.shape, -jnp.inf, F32)

    def scores(j, qv, bias):
        start = pl.multiple_of(j * tk, tk)
        for c in range(2):
            s = lax.dot_general(qv[c], ka_ref[c, pl.ds(start, tk), :], NT_DIMS,
                                preferred_element_type=F32)
            if bias is not None:
                s = s - bias
            s_ref[c, j] = s
            mt = m_ref[c]
            for l in range(nl):
                mt = jnp.maximum(mt, s[:, l * LANES:(l + 1) * LANES])
            m_ref[c] = mt

    jd = (i * tq) // tk
    lax.fori_loop(0, jd, lambda j, cr: (scores(j, q_left, None), cr)[1], 0)
    qpos = (i * tq + lax.broadcasted_iota(jnp.int32, (tq, 1), 0)).astype(F32)
    kpos = (jd * tk + lax.broadcasted_iota(jnp.int32, (1, tk), 1)).astype(F32)
    scores(jd, q_diag, slope * jnp.abs(qpos - kpos))
    lax.fori_loop(jd + 1, nkv, lambda j, cr: (scores(j, q_right, None), cr)[1], 0)

    for c in range(2):
        m_ref[c] = jnp.broadcast_to(jnp.max(m_ref[c], axis=-1, keepdims=True), (tq, LANES))
    l_ref[...] = jnp.zeros(l_ref.shape, F32)
    acc_ref[...] = jnp.zeros(acc_ref.shape, F32)

    def accumulate(j, carry):
        vb = v_ref[pl.ds(pl.multiple_of(j * tk, tk), tk), :]
        for c in range(2):
            s = s_ref[c, j]
            mb = m_ref[c]
            ps = [jnp.exp(s[:, l * LANES:(l + 1) * LANES] - mb) for l in range(nl)]
            lsum = l_ref[c]
            for p in ps:
                lsum = lsum + p
            l_ref[c] = lsum
            acc_ref[c] += jnp.dot(jnp.concatenate(ps, axis=1).astype(BF16), vb,
                                  preferred_element_type=F32)
        return carry

    lax.fori_loop(0, nkv, accumulate, 0)
    l0 = jnp.sum(l_ref[0], axis=-1, keepdims=True)
    l1 = jnp.sum(l_ref[1], axis=-1, keepdims=True)
    o = acc_ref[0] / l0 - lam * (acc_ref[1] / l1)
    y = o * lax.rsqrt(jnp.mean(o * o, axis=-1, keepdims=True) + RMS_EPS) * g_ref[...]
    o_ref[...] = (y * (1.0 - linit)).astype(o_ref.dtype)


ATTN_SCORE_BYTES = 32 * 1024 * 1024


def _attention(q, k, v, lamv, slopes, subln_g, linit, tk=1024):
    b, s, aw = q.shape
    hv = ATTN_HEAD_V
    tq = min(512, ATTN_SCORE_BYTES // (2 * 4 * s))
    assert tk % tq == 0 and s % tk == 0
    qspec = pl.BlockSpec((None, tq, hv), lambda bi, hi, qi: (bi, qi, hi))
    kvspec = pl.BlockSpec((None, s, hv), lambda bi, hi, qi: (bi, 0, hi))
    return pl.pallas_call(
        functools.partial(_attn_kernel, tq=tq, tk=tk, linit=linit),
        grid=(b, aw // hv, s // tq),
        in_specs=[pl.BlockSpec((4, ATTN_HEAD_QK), lambda bi, hi, qi: (0, 0)),
                  pl.BlockSpec(memory_space=pltpu.SMEM),
                  qspec, kvspec, kvspec,
                  pl.BlockSpec((1, hv), lambda bi, hi, qi: (0, 0))],
        out_specs=qspec,
        out_shape=jax.ShapeDtypeStruct((b, s, aw), BF16),
        scratch_shapes=[pltpu.VMEM((2, s, hv), BF16), pltpu.VMEM((2, s // tk, tq, tk), F32),
                        pltpu.VMEM((2, tq, LANES), F32), pltpu.VMEM((2, tq, LANES), F32),
                        pltpu.VMEM((2, tq, hv), F32)],
        compiler_params=_params(("parallel", "parallel", "arbitrary"), vmem=BIG_VMEM_LIMIT),
        name="attn",
    )(lamv, slopes, q, k, v, subln_g)


DFT_FINE = 64


def _fourier_kernel(ca_ref, nsa_ref, cr_ref, nsr_ref, a_ref, b_ref, bias_ref, f_ref, acc_ref):
    kk = pl.program_id(2)

    @pl.when(kk == 0)
    def _():
        acc_ref[...] = jnp.zeros(acc_ref.shape, F32)

    cr = cr_ref[...]
    nsr = nsr_ref[...]
    cos_rows, nsin_rows = [], []
    for a in range(ca_ref.shape[0]):
        ca = ca_ref[a:a + 1, :]
        nsa = nsa_ref[a:a + 1, :]
        cos_rows.append((ca * cr - nsa * nsr).astype(BF16))
        nsin_rows.append((nsa * cr + ca * nsr).astype(BF16))
    acc_ref[...] += (jnp.dot(jnp.concatenate(cos_rows, axis=0), a_ref[...], preferred_element_type=F32)
                     + jnp.dot(jnp.concatenate(nsin_rows, axis=0), b_ref[...],
                               preferred_element_type=F32))

    @pl.when(kk == pl.num_programs(2) - 1)
    def _():
        f_ref[...] = (acc_ref[...] + bias_ref[...]).astype(f_ref.dtype)


def _dft_tables(s):
    k = jnp.arange(s, dtype=jnp.int32)[None, :]

    def cos_nsin(rows, scale):
        ang = ((rows[:, None] * k) % s).astype(F32) * (2.0 * math.pi / s)
        return jnp.cos(ang) * scale, jnp.sin(ang) * (-scale)

    coarse = cos_nsin(jnp.arange(s // DFT_FINE, dtype=jnp.int32) * DFT_FINE, s ** -0.5)
    fine = cos_nsin(jnp.arange(DFT_FINE, dtype=jnp.int32), 1.0)
    return coarse + fine


def _fourier(a, bm, bias, tm=1024, tk=1024):
    b, s, fw = a.shape
    ca, nsa, cr, nsr = _dft_tables(s)
    cspec = pl.BlockSpec((tm // DFT_FINE, tk), lambda bi, i, kk: (i, kk))
    fspec = pl.BlockSpec((DFT_FINE, tk), lambda bi, i, kk: (0, kk))
    xspec = pl.BlockSpec((None, tk, fw), lambda bi, i, kk: (bi, kk, 0))
    return pl.pallas_call(
        _fourier_kernel,
        grid=(b, s // tm, s // tk),
        in_specs=[cspec, cspec, fspec, fspec, xspec, xspec,
                  pl.BlockSpec((1, fw), lambda bi, i, kk: (0, 0))],
        out_specs=pl.BlockSpec((None, tm, fw), lambda bi, i, kk: (bi, i, 0)),
        out_shape=jax.ShapeDtypeStruct((b, s, fw), BF16),
        scratch_shapes=[pltpu.VMEM((tm, fw), F32)],
        compiler_params=_params(("parallel", "parallel", "arbitrary")),
        name="fourier",
    )(ca, nsa, cr, nsr, a, bm, bias)


def _outproj_kernel(o_ref, f_ref, x_ref, gi_ref, bi_ref, wo_ref, g1_ref, b1_ref, wq_ref,
                    x1_ref, x1b_ref, qp_ref, *, alpha):
    aw = o_ref.shape[1]
    mix = (jnp.dot(o_ref[...], wo_ref[:aw, :], preferred_element_type=F32)
           + jnp.dot(f_ref[...], wo_ref[aw:, :], preferred_element_type=F32))
    xn = _layer_norm(x_ref[...], gi_ref[...], bi_ref[...])
    x1 = _layer_norm(alpha * xn + mix, g1_ref[...], b1_ref[...])
    x1_ref[...] = x1
    x1b = x1.astype(BF16)
    x1b_ref[...] = x1b
    qp = jnp.dot(x1b, wq_ref[...], preferred_element_type=F32).astype(BF16)
    for h in range(qp_ref.shape[0]):
        qp_ref[h] = qp[:, h * LANES:(h + 1) * LANES]


def _out_proj(o2, f2, x2, gi, bi, wo, g1, b1, wq, alpha, tm=512):
    t, d = x2.shape
    row = lambda i: (i, 0)
    full = lambda i: (0, 0)
    vec = pl.BlockSpec((1, d), full)
    return pl.pallas_call(
        functools.partial(_outproj_kernel, alpha=alpha),
        grid=(t // tm,),
        in_specs=[pl.BlockSpec((tm, o2.shape[1]), row), pl.BlockSpec((tm, f2.shape[1]), row),
                  pl.BlockSpec((tm, d), row), vec, vec, pl.BlockSpec(wo.shape, full), vec, vec,
                  pl.BlockSpec(wq.shape, full)],
        out_specs=[pl.BlockSpec((tm, d), row), pl.BlockSpec((tm, d), row),
                   pl.BlockSpec((PEER_HEADS, tm, LANES), lambda i: (0, i, 0))],
        out_shape=[jax.ShapeDtypeStruct((t, d), F32), jax.ShapeDtypeStruct((t, d), BF16),
                   jax.ShapeDtypeStruct((PEER_HEADS, t, LANES), BF16)],
        compiler_params=_params(("parallel",)),
        name="out_proj",
    )(o2, f2, x2, gi, bi, wo, g1, b1, wq)


def _topk_cols(s, pos, k, payload=None):
    return _topk_cols_multi([s], pos, k, payload)[0]


def _topk_cols_multi(scores, pos, k, payload=None):
    big = jnp.float32(1e9)
    scores = list(scores)
    outs = [([], []) for _ in scores]
    for _ in range(k):
        for n, s in enumerate(scores):
            m = jnp.max(s, axis=0, keepdims=True)
            am = jnp.min(jnp.where(s == m, pos, big), axis=0, keepdims=True)
            hit = pos == am
            outs[n][0].append(m)
            if payload is None:
                outs[n][1].append(am)
            else:
                outs[n][1].append(jnp.max(jnp.where(hit, payload, -1.0), axis=0, keepdims=True))
            scores[n] = jnp.where(hit, -jnp.inf, s)
    return outs


def _pair_candidates(s1r, i1r, s2r, i2r):
    kk = PEER_TOPK
    tb = s1r[0].shape[1]
    r16 = lax.broadcasted_iota(jnp.int32, (kk, tb), 0).astype(F32)
    r8 = lax.broadcasted_iota(jnp.int32, (SUBLANES, tb), 0).astype(F32)
    nk = float(PEER_N_KEYS)
    ninf = -jnp.inf
    cat = lambda rows: jnp.concatenate(rows, axis=0)
    s1, i1, s2, i2 = cat(s1r), cat(i1r), cat(s2r), cat(i2r)
    s1l, i1l, s2l, i2l = (cat(r[:SUBLANES]) for r in (s1r, i1r, s2r, i2r))
    groups = []
    groups.append((s1 + s2r[0], kk * r16, i1 * nk + i2r[0], None))
    groups.append((s1r[0] + s2, r16, i1r[0] * nk + i2, r16 >= 1.0))
    groups.append((s1l + s2r[1], kk * r8 + 1.0, i1l * nk + i2r[1], r8 >= 1.0))
    groups.append((s1r[1] + s2l, kk + r8, i1r[1] * nk + i2l, r8 >= 2.0))
    for b in (2, 3, 4):
        amax = kk // (b + 1) - 1
        groups.append((s1l + s2r[b], kk * r8 + float(b), i1l * nk + i2r[b],
                       (r8 >= 2.0) & (r8 <= float(amax))))
    val = jnp.concatenate([v if ok is None else jnp.where(ok, v, ninf) for v, _, _, ok in groups], axis=0)
    pos = jnp.concatenate([p for _, p, _, _ in groups], axis=0)
    eid = jnp.concatenate([e for _, _, e, _ in groups], axis=0)
    return val, pos, eid


def _retrieve(q, sk):
    tb = q.shape[0]
    key_pos = lax.broadcasted_iota(jnp.int32, (PEER_N_KEYS, tb), 0).astype(F32)
    sc = [lax.dot_general(sk[c], q, NT_DIMS, preferred_element_type=F32)
          for c in range(2)]
    (s1, i1), (s2, i2) = _topk_cols_multi(sc, key_pos, PEER_TOPK)
    val, pos, eid = _pair_candidates(s1, i1, s2, i2)
    bestr, eselr = _topk_cols(val, pos, PEER_TOPK, payload=eid)
    best = jnp.concatenate(bestr, axis=0)
    ex = jnp.exp(best - bestr[0])
    return jnp.concatenate(eselr, axis=0), ex / jnp.sum(ex, axis=0, keepdims=True)


TOK_UNROLL = 8
U_BLOCK = 256


V_BLOCK = 1024


def _block_transposed(table, block):
    n, d = table.shape
    return table.astype(BF16).reshape(n // block, block, d).transpose(0, 2, 1)


ROUTE_TOK = 256
SCORE_SLOTS = 4


def _route_u_kernel(qp_ref, sk_ref, x_ref, ut_ref, e_ref, coef_ref,
                    et_ref, gt_ref, gprev_ref, blk_ref, lane_ref, act_ref, s_ref):
    i = pl.program_id(0)
    tb = x_ref.shape[0]
    nsub = tb // ROUTE_TOK
    sub = U_BLOCK // LANES
    nblk = ut_ref.shape[0]
    ngroups = nblk // SCORE_SLOTS
    units_per_group = (nsub * PEER_HEADS) // ngroups

    @pl.when(i == 0)
    def _():
        gprev_ref[...] = jnp.zeros(gprev_ref.shape, F32)
        blk_ref[...] = jnp.zeros(blk_ref.shape, jnp.int32)
        lane_ref[...] = jnp.zeros(lane_ref.shape, jnp.int32)

    act_ref[...] = jnp.zeros(act_ref.shape, F32)

    def retrieve_unit(u):
        h = u % PEER_HEADS
        sb = u // PEER_HEADS
        q = qp_ref[h, pl.ds(pl.multiple_of(sb * ROUTE_TOK, ROUTE_TOK), ROUTE_TOK), :]
        eid, gates = _retrieve(q, sk_ref[h])
        r0 = pl.multiple_of(h * PEER_TOPK, PEER_TOPK)
        et_ref[sb, pl.ds(r0, PEER_TOPK), :] = eid
        gt_ref[sb, pl.ds(r0, PEER_TOPK), :] = gates

    def score(b, slot):
        s_ref[slot] = jnp.dot(x_ref[...], ut_ref[b], preferred_element_type=F32)

    def gather(b, slot):
        lane = lane_ref[...]
        blk = blk_ref[...]
        act = act_ref[...]
        for h in range(sub):
            got = jnp.take_along_axis(s_ref[slot, :, h * LANES:(h + 1) * LANES], lane, axis=1)
            act = jnp.where(blk == b * sub + h, got, act)
        act_ref[...] = act

    def group(g, last):
        for r in range(SCORE_SLOTS):
            b = SCORE_SLOTS * g + r
            if not (last and r + 1 == SCORE_SLOTS):
                score(b + 1, (r + 1) % SCORE_SLOTS)
            gather(b, r)
            if r < units_per_group:
                retrieve_unit(units_per_group * g + r)

    score(0, 0)
    lax.fori_loop(0, ngroups - 1, lambda g, c: (group(g, False), c)[1], 0)
    group(ngroups - 1, True)
    coef_ref[...] = gprev_ref[...] * jax.nn.gelu(act_ref[...])

    for sb in range(nsub):
        rows = pl.ds(sb * ROUTE_TOK, ROUTE_TOK)
        e_nat = et_ref[sb].T.astype(jnp.int32)
        e_ref[rows, :] = e_nat
        blk_ref[rows, :] = e_nat // LANES
        lane_ref[rows, :] = e_nat % LANES
        gprev_ref[rows, :] = gt_ref[sb].T


def _route_u(qp_h, sk_ext, x1b, ut, tb=1024):
    t, d = x1b.shape
    nb = t // tb
    cur = lambda i: (jnp.minimum(i, nb - 1), 0)
    prev = lambda i: (jnp.maximum(i - 1, 0), 0)
    assert (tb // ROUTE_TOK * PEER_HEADS) % (ut.shape[0] // SCORE_SLOTS) == 0
    return pl.pallas_call(
        _route_u_kernel,
        grid=(nb + 1,),
        in_specs=[pl.BlockSpec((PEER_HEADS, tb, LANES), lambda i: (0, jnp.minimum(i, nb - 1), 0)),
                  pl.BlockSpec(memory_space=pltpu.VMEM),
                  pl.BlockSpec((tb, d), prev), pl.BlockSpec(memory_space=pltpu.VMEM)],
        out_specs=[pl.BlockSpec((tb, PICKS), cur), pl.BlockSpec((tb, PICKS), prev)],
        out_shape=[jax.ShapeDtypeStruct((t, PICKS), jnp.int32), jax.ShapeDtypeStruct((t, PICKS), F32)],
        scratch_shapes=[pltpu.VMEM((tb // ROUTE_TOK, PICKS, ROUTE_TOK), F32),
                        pltpu.VMEM((tb // ROUTE_TOK, PICKS, ROUTE_TOK), F32),
                        pltpu.VMEM((tb, PICKS), F32),
                        pltpu.VMEM((tb, PICKS), jnp.int32), pltpu.VMEM((tb, PICKS), jnp.int32),
                        pltpu.VMEM((tb, PICKS), F32), pltpu.VMEM((SCORE_SLOTS, tb, U_BLOCK), F32)],
        compiler_params=_params(("arbitrary",)),
        name="route_u",
    )(qp_h, sk_ext, x1b, ut)


def _peer_v_kernel(e_ref, coef_ref, x_ref, g_ref, b_ref, vt_ref, y_ref, w_ref, acc_ref, *, alpha):
    tb, d = x_ref.shape
    pitch = w_ref.shape[0] // LANES
    rows = lax.broadcasted_iota(jnp.int32, (LANES, PICKS), 0)

    def scatter_group(first):
        eg = e_ref[pl.ds(first, TOK_UNROLL), :]
        cg = coef_ref[pl.ds(first, TOK_UNROLL), :]
        ig = eg // LANES
        jg = eg % LANES
        for tt in range(TOK_UNROLL):
            ib = jnp.broadcast_to(ig[tt:tt + 1, :], rows.shape)
            jb = jnp.broadcast_to(jg[tt:tt + 1, :], rows.shape)
            cb = jnp.broadcast_to(cg[tt:tt + 1, :], rows.shape)
            hot_i = jnp.where(ib == rows, 1.0, 0.0).astype(BF16)
            hot_j = jnp.where(jb == rows, cb, 0.0).astype(BF16)
            wt = lax.dot_general(hot_i, hot_j, NT_DIMS, preferred_element_type=F32)
            w_ref[pl.ds(first + tt, LANES, stride=pitch), :] = wt

    groups = 4

    def scatter(it, carry):
        for gi in range(groups):
            scatter_group(pl.multiple_of((it * groups + gi) * TOK_UNROLL, TOK_UNROLL))
        return carry

    lax.fori_loop(0, tb // (TOK_UNROLL * groups), scatter, 0)

    acc_ref[...] = jnp.zeros(acc_ref.shape, F32)
    sub = vt_ref.shape[2] // LANES

    def block(bp, carry):
        parts = [w_ref[pl.ds(pl.multiple_of((bp * sub + h) * pitch, SUBLANES), tb), :]
                 for h in range(sub)]
        wp = jnp.concatenate(parts, axis=1).astype(BF16)
        acc_ref[...] += lax.dot_general(vt_ref[bp], wp, NT_DIMS, preferred_element_type=F32)
        return carry

    lax.fori_loop(0, vt_ref.shape[0], block, 0)
    y_ref[...] = _layer_norm(alpha * x_ref[...] + acc_ref[...].T, g_ref[...], b_ref[...])


def _peer_v(eidx, coef, x1, g2, b2, vt, alpha, tb=256):
    t, d = x1.shape
    row = lambda i: (i, 0)
    vec = pl.BlockSpec((1, d), lambda i: (0, 0))
    return pl.pallas_call(
        functools.partial(_peer_v_kernel, alpha=alpha),
        grid=(t // tb,),
        in_specs=[pl.BlockSpec((tb, PICKS), row), pl.BlockSpec((tb, PICKS), row),
                  pl.BlockSpec((tb, d), row), vec, vec, pl.BlockSpec(memory_space=pltpu.VMEM)],
        out_specs=pl.BlockSpec((tb, d), row),
        out_shape=jax.ShapeDtypeStruct((t, d), F32),
        scratch_shapes=[pltpu.VMEM((LANES * (tb + SUBLANES), LANES), F32), pltpu.VMEM((d, tb), F32)],
        compiler_params=_params(("arbitrary",), vmem=BIG_VMEM_LIMIT),
        name="peer_v",
    )(eidx, coef, x1, g2, b2, vt)


def _trunk(x, p):
    b, s, d = x.shape
    t = b * s
    depth = p["w_in"].shape[0]
    alpha = (2 * depth) ** 0.25
    x2 = x.reshape(t, d)
    cur = x2
    gi, bi = p["ln_in_g"].reshape(1, d), p["ln_in_b"].reshape(1, d)
    for l in range(depth):
        assert l == 0, "single-layer trunk"
        linit = 0.8 - 0.6 * math.exp(-0.3 * l)
        aw = N_ATTN_HEADS * ATTN_HEAD_V
        w_in = p["w_in"][l]
        wab = _prep(w_in[:, 3 * aw:], p["w_fourier"][l])
        q, k, v, a, bm = _in_proj(cur, gi, bi, w_in[:, :3 * aw].astype(BF16), wab)
        lamv = jnp.stack([p["lambda_q1"][l], p["lambda_k1"][l], p["lambda_q2"][l], p["lambda_k2"][l]])
        hh = jnp.arange(1, N_ATTN_HEADS + 1, dtype=F32)
        slopes = jnp.exp2(-8.0 * hh / N_ATTN_HEADS)
        o = _attention(q.reshape(b, s, aw), k.reshape(b, s, aw), v.reshape(b, s, aw), lamv, slopes,
                       p["subln_g"][l].reshape(1, ATTN_HEAD_V), linit)
        f = _fourier(a.reshape(b, s, -1), bm.reshape(b, s, -1), p["b_fourier"][l].reshape(1, -1))
        x1, x1b, qp_h = _out_proj(o.reshape(t, aw), f.reshape(t, -1), cur, gi, bi,
                                  p["w_out"][l].astype(BF16), p["ln1_g"][l].reshape(1, d),
                                  p["ln1_b"][l].reshape(1, d), p["peer_wq"][l].astype(BF16), alpha)
        sk = p["peer_subkeys"][l]
        z = jnp.zeros_like(sk)
        sk_ext = jnp.stack([jnp.concatenate([sk[:, 0], z[:, 0]], axis=-1),
                            jnp.concatenate([z[:, 1], sk[:, 1]], axis=-1)], axis=1).astype(BF16)
        eidx, coef = _route_u(qp_h, sk_ext, x1b, _block_transposed(p["peer_u"][l], U_BLOCK))
        cur = _peer_v(eidx, coef, x1, p["ln2_g"][l].reshape(1, d), p["ln2_b"][l].reshape(1, d),
                      _block_transposed(p["peer_v"][l], V_BLOCK), alpha)
    return cur.reshape(b, s, d)


def kernel(x_prompt, x_sample, ln_in_g, ln_in_b, w_in, lambda_q1, lambda_k1, lambda_q2, lambda_k2,
           subln_g, w_fourier, b_fourier, w_out, ln1_g, ln1_b, peer_wq, peer_subkeys, peer_u, peer_v,
           ln2_g, ln2_b):
    p = dict(ln_in_g=ln_in_g, ln_in_b=ln_in_b, w_in=w_in, lambda_q1=lambda_q1, lambda_k1=lambda_k1,
             lambda_q2=lambda_q2, lambda_k2=lambda_k2, subln_g=subln_g, w_fourier=w_fourier,
             b_fourier=b_fourier, w_out=w_out, ln1_g=ln1_g, ln1_b=ln1_b, peer_wq=peer_wq,
             peer_subkeys=peer_subkeys, peer_u=peer_u, peer_v=peer_v, ln2_g=ln2_g, ln2_b=ln2_b)
    return (_trunk(x_prompt, p), _trunk(x_sample, p))
```

```python
import functools
import math

import jax
import jax.numpy as jnp
from jax import lax
from jax.experimental import pallas as pl
from jax.experimental.pallas import tpu as pltpu

F32 = jnp.float32
BF16 = jnp.bfloat16

LN_EPS = 1e-5
RMS_EPS = 1e-5
N_ATTN_HEADS = 4
ATTN_HEAD_V = 128
ATTN_HEAD_QK = 64
N_FOURIER_GROUPS = 4
FOURIER_GROUP = 128
PEER_HEADS = 8
PEER_N_KEYS = 128
PEER_HALF = 64
PEER_TOPK = 16
PICKS = PEER_HEADS * PEER_TOPK

LANES = 128
SUBLANES = 8
VMEM_LIMIT = 56 * 1024 * 1024
BIG_VMEM_LIMIT = 60 * 1024 * 1024

NT_DIMS = (((1,), (1,)), ((), ()))
LOG2E = 1.4426950408889634


def _params(sem, vmem=VMEM_LIMIT):
    return pltpu.CompilerParams(dimension_semantics=sem, vmem_limit_bytes=vmem)


def _layer_norm(x, g, b):
    mu = jnp.mean(x, axis=-1, keepdims=True)
    xc = x - mu
    var = jnp.mean(xc * xc, axis=-1, keepdims=True)
    return xc * lax.rsqrt(var + LN_EPS) * g + b


def _prep_kernel(win_f_ref, wf_ref, c_ref, s_ref, wab_ref):
    hp = lax.Precision.HIGHEST
    for g in range(N_FOURIER_GROUPS):
        wg = wf_ref[g]
        cw = jnp.dot(c_ref[...], wg, precision=hp, preferred_element_type=F32)
        sw = jnp.dot(s_ref[...], wg, precision=hp, preferred_element_type=F32)
        wi = win_f_ref[:, g * FOURIER_GROUP:(g + 1) * FOURIER_GROUP]
        a = jnp.dot(wi, cw, precision=hp, preferred_element_type=F32)
        b = jnp.dot(wi, sw, precision=hp, preferred_element_type=F32)
        wab_ref[:, g * FOURIER_GROUP:(g + 1) * FOURIER_GROUP] = a.astype(BF16)
        wab_ref[:, (N_FOURIER_GROUPS + g) * FOURIER_GROUP:
                (N_FOURIER_GROUPS + g + 1) * FOURIER_GROUP] = b.astype(BF16)


def _prep(win_f, w_fourier):
    d = win_f.shape[0]
    n = FOURIER_GROUP
    idx = jnp.arange(n, dtype=jnp.int32)
    ang = ((idx[:, None] * idx[None, :]) % n).astype(F32) * (2.0 * math.pi / n)
    cmat = jnp.cos(ang) * (n ** -0.5)
    smat = jnp.sin(ang) * (n ** -0.5)
    return pl.pallas_call(
        _prep_kernel,
        out_shape=jax.ShapeDtypeStruct((d, 2 * N_FOURIER_GROUPS * n), BF16),
        name="prep",
    )(win_f, w_fourier, cmat, smat)


def _inproj_kernel(x_ref, g_ref, b_ref, wqkv_ref, wab_ref, q_ref, k_ref, v_ref, a_ref, bb_ref):
    xn = _layer_norm(x_ref[...], g_ref[...], b_ref[...]).astype(BF16)
    h = jnp.dot(xn, wqkv_ref[...], preferred_element_type=F32)
    aw = q_ref.shape[1]
    q_ref[...] = (h[:, :aw] * (ATTN_HEAD_QK ** -0.5)).astype(BF16)
    k_ref[...] = h[:, aw:2 * aw].astype(BF16)
    v_ref[...] = h[:, 2 * aw:3 * aw].astype(BF16)
    ab = jnp.dot(xn, wab_ref[...], preferred_element_type=F32)
    fw = a_ref.shape[1]
    a_ref[...] = ab[:, :fw].astype(BF16)
    bb_ref[...] = ab[:, fw:].astype(BF16)


def _in_proj(x2, g, b, wqkv, wab, tm=512):
    t, d = x2.shape
    aw = wqkv.shape[1] // 3
    fw = wab.shape[1] // 2
    row = lambda i: (i, 0)
    full = lambda i: (0, 0)
    outs = [jax.ShapeDtypeStruct((t, aw), BF16)] * 3 + [jax.ShapeDtypeStruct((t, fw), BF16)] * 2
    return pl.pallas_call(
        _inproj_kernel,
        grid=(t // tm,),
        in_specs=[pl.BlockSpec((tm, d), row), pl.BlockSpec((1, d), full), pl.BlockSpec((1, d), full),
                  pl.BlockSpec(wqkv.shape, full), pl.BlockSpec(wab.shape, full)],
        out_specs=[pl.BlockSpec((tm, aw), row)] * 3 + [pl.BlockSpec((tm, fw), row)] * 2,
        out_shape=outs,
        compiler_params=_params(("parallel",)),
        name="in_proj",
    )(x2, g, b, wqkv, wab)


def _attn_kernel(lamv_ref, slope_ref, q_ref, k_ref, v_ref, g_ref, o_ref,
                 ka_ref, s_ref, m_ref, l_ref, acc_ref, *, tq, tk, linit):
    h = pl.program_id(1)
    i = pl.program_id(2)
    s_len = k_ref.shape[0]
    nkv = s_len // tk
    nl = tk // LANES
    slope = slope_ref[h]
    lv = lamv_ref[...]
    lam = (jnp.exp(jnp.sum(lv[0:1] * lv[1:2], axis=-1, keepdims=True))
           - jnp.exp(jnp.sum(lv[2:3] * lv[3:4], axis=-1, keepdims=True)) + linit)
    dq = ATTN_HEAD_QK

    def aug_lanes(shape, first_pos, c):
        lane = lax.broadcasted_iota(jnp.int32, shape, 1)
        pos = first_pos + lax.broadcasted_iota(jnp.int32, shape, 0)
        hi = (pos // dq).astype(F32) * (slope * dq)
        lo = (pos % dq).astype(F32) * slope
        base = dq * (1 - c)
        keep = (lane < dq) if c == 0 else (lane >= dq)
        return lane - base, hi, lo, keep

    @pl.when(i == 0)
    def _():
        k = k_ref[...]
        for c in range(2):
            rel, hi, lo, keep = aug_lanes(k.shape, 0, c)
            aug = jnp.where(rel == 0, hi, jnp.where(rel == 1, lo,
                            jnp.where((rel == 2) | (rel == 3), 1.0, 0.0)))
            ka_ref[c] = jnp.where(keep, k, aug.astype(BF16))

    q = q_ref[...]
    q_left, q_right, q_diag = [], [], []
    for c in range(2):
        rel, hi, lo, keep = aug_lanes(q.shape, i * tq, c)
        aug = jnp.where((rel == 0) | (rel == 1), 1.0,
                        jnp.where(rel == 2, -hi, jnp.where(rel == 3, -lo, 0.0)))
        q_left.append(jnp.where(keep, q, aug.astype(BF16)))
        q_right.append(jnp.where(keep, q, (-aug).astype(BF16)))
        q_diag.append(jnp.where(keep, q, jnp.zeros_like(q)))

    m_ref[...] = jnp.full(m_ref.shape, -jnp.inf, F32)

    def scores(j, qv, bias):
        start = pl.multiple_of(j * tk, tk)
        for c in range(2):
            s = lax.dot_general(qv[c], ka_ref[c, pl.ds(start, tk), :], NT_DIMS,
                                preferred_element_type=F32)
            if bias is not None:
                s = s - bias
            s = s * LOG2E
            s_ref[c, j] = s
            mt = m_ref[c]
            for l in range(nl):
                mt = jnp.maximum(mt, s[:, l * LANES:(l + 1) * LANES])
            m_ref[c] = mt

    jd = (i * tq) // tk
    lax.fori_loop(0, jd, lambda j, cr: (scores(j, q_left, None), cr)[1], 0)
    qpos = (i * tq + lax.broadcasted_iota(jnp.int32, (tq, 1), 0)).astype(F32)
    kpos = (jd * tk + lax.broadcasted_iota(jnp.int32, (1, tk), 1)).astype(F32)
    scores(jd, q_diag, slope * jnp.abs(qpos - kpos))
    lax.fori_loop(jd + 1, nkv, lambda j, cr: (scores(j, q_right, None), cr)[1], 0)

    for c in range(2):
        m_ref[c] = jnp.broadcast_to(jnp.max(m_ref[c], axis=-1, keepdims=True), (tq, LANES))
    l_ref[...] = jnp.zeros(l_ref.shape, F32)
    acc_ref[...] = jnp.zeros(acc_ref.shape, F32)

    def accumulate(j, carry):
        vb = v_ref[pl.ds(pl.multiple_of(j * tk, tk), tk), :]
        for c in range(2):
            s = s_ref[c, j]
            mb = m_ref[c]
            ps = [jnp.exp2(s[:, l * LANES:(l + 1) * LANES] - mb) for l in range(nl)]
            lsum = l_ref[c]
            for p in ps:
                lsum = lsum + p
            l_ref[c] = lsum
            acc_ref[c] += jnp.dot(jnp.concatenate(ps, axis=1).astype(BF16), vb,
                                  preferred_element_type=F32)
        return carry

    lax.fori_loop(0, nkv, accumulate, 0)
    l0 = jnp.sum(l_ref[0], axis=-1, keepdims=True)
    l1 = jnp.sum(l_ref[1], axis=-1, keepdims=True)
    o = acc_ref[0] / l0 - lam * (acc_ref[1] / l1)
    y = o * lax.rsqrt(jnp.mean(o * o, axis=-1, keepdims=True) + RMS_EPS) * g_ref[...]
    o_ref[...] = (y * (1.0 - linit)).astype(o_ref.dtype)


ATTN_SCORE_BYTES = 32 * 1024 * 1024


def _attention(q, k, v, lamv, slopes, subln_g, linit, tk=1024):
    b, s, aw = q.shape
    hv = ATTN_HEAD_V
    tq = min(512, ATTN_SCORE_BYTES // (2 * 4 * s))
    assert tk % tq == 0 and s % tk == 0
    qspec = pl.BlockSpec((None, tq, hv), lambda bi, hi, qi: (bi, qi, hi))
    kvspec = pl.BlockSpec((None, s, hv), lambda bi, hi, qi: (bi, 0, hi))
    return pl.pallas_call(
        functools.partial(_attn_kernel, tq=tq, tk=tk, linit=linit),
        grid=(b, aw // hv, s // tq),
        in_specs=[pl.BlockSpec((4, ATTN_HEAD_QK), lambda bi, hi, qi: (0, 0)),
                  pl.BlockSpec(memory_space=pltpu.SMEM),
                  qspec, kvspec, kvspec,
                  pl.BlockSpec((1, hv), lambda bi, hi, qi: (0, 0))],
        out_specs=qspec,
        out_shape=jax.ShapeDtypeStruct((b, s, aw), BF16),
        scratch_shapes=[pltpu.VMEM((2, s, hv), BF16), pltpu.VMEM((2, s // tk, tq, tk), F32),
                        pltpu.VMEM((2, tq, LANES), F32), pltpu.VMEM((2, tq, LANES), F32),
                        pltpu.VMEM((2, tq, hv), F32)],
        compiler_params=_params(("parallel", "parallel", "arbitrary"), vmem=BIG_VMEM_LIMIT),
        name="attn",
    )(lamv, slopes, q, k, v, subln_g)


DFT_FINE = 64


def _fourier_kernel(ca_ref, nsa_ref, cr_ref, nsr_ref, a_ref, b_ref, bias_ref, f_ref, acc_ref):
    kk = pl.program_id(2)

    @pl.when(kk == 0)
    def _():
        acc_ref[...] = jnp.zeros(acc_ref.shape, F32)

    cr = cr_ref[...]
    nsr = nsr_ref[...]
    cos_rows, nsin_rows = [], []
    for a in range(ca_ref.shape[0]):
        ca = ca_ref[a:a + 1, :]
        nsa = nsa_ref[a:a + 1, :]
        cos_rows.append((ca * cr - nsa * nsr).astype(BF16))
        nsin_rows.append((nsa * cr + ca * nsr).astype(BF16))
    acc_ref[...] += (jnp.dot(jnp.concatenate(cos_rows, axis=0), a_ref[...], preferred_element_type=F32)
                     + jnp.dot(jnp.concatenate(nsin_rows, axis=0), b_ref[...],
                               preferred_element_type=F32))

    @pl.when(kk == pl.num_programs(2) - 1)
    def _():
        f_ref[...] = (acc_ref[...] + bias_ref[...]).astype(f_ref.dtype)


def _dft_tables(s):
    k = jnp.arange(s, dtype=jnp.int32)[None, :]

    def cos_nsin(rows, scale):
        ang = ((rows[:, None] * k) % s).astype(F32) * (2.0 * math.pi / s)
        return jnp.cos(ang) * scale, jnp.sin(ang) * (-scale)

    coarse = cos_nsin(jnp.arange(s // DFT_FINE, dtype=jnp.int32) * DFT_FINE, s ** -0.5)
    fine = cos_nsin(jnp.arange(DFT_FINE, dtype=jnp.int32), 1.0)
    return coarse + fine


def _fourier(a, bm, bias, tm=1024, tk=1024):
    b, s, fw = a.shape
    ca, nsa, cr, nsr = _dft_tables(s)
    cspec = pl.BlockSpec((tm // DFT_FINE, tk), lambda bi, i, kk: (i, kk))
    fspec = pl.BlockSpec((DFT_FINE, tk), lambda bi, i, kk: (0, kk))
    xspec = pl.BlockSpec((None, tk, fw), lambda bi, i, kk: (bi, kk, 0))
    return pl.pallas_call(
        _fourier_kernel,
        grid=(b, s // tm, s // tk),
        in_specs=[cspec, cspec, fspec, fspec, xspec, xspec,
                  pl.BlockSpec((1, fw), lambda bi, i, kk: (0, 0))],
        out_specs=pl.BlockSpec((None, tm, fw), lambda bi, i, kk: (bi, i, 0)),
        out_shape=jax.ShapeDtypeStruct((b, s, fw), BF16),
        scratch_shapes=[pltpu.VMEM((tm, fw), F32)],
        compiler_params=_params(("parallel", "parallel", "arbitrary")),
        name="fourier",
    )(ca, nsa, cr, nsr, a, bm, bias)


def _outproj_kernel(o_ref, f_ref, x_ref, gi_ref, bi_ref, wo_ref, g1_ref, b1_ref, wq_ref,
                    x1_ref, x1b_ref, qp_ref, *, alpha):
    aw = o_ref.shape[1]
    mix = (jnp.dot(o_ref[...], wo_ref[:aw, :], preferred_element_type=F32)
           + jnp.dot(f_ref[...], wo_ref[aw:, :], preferred_element_type=F32))
    xn = _layer_norm(x_ref[...], gi_ref[...], bi_ref[...])
    x1 = _layer_norm(alpha * xn + mix, g1_ref[...], b1_ref[...])
    x1_ref[...] = x1
    x1b = x1.astype(BF16)
    x1b_ref[...] = x1b
    qp = jnp.dot(x1b, wq_ref[...], preferred_element_type=F32).astype(BF16)
    for h in range(qp_ref.shape[0]):
        qp_ref[h] = qp[:, h * LANES:(h + 1) * LANES]


def _out_proj(o2, f2, x2, gi, bi, wo, g1, b1, wq, alpha, tm=512):
    t, d = x2.shape
    row = lambda i: (i, 0)
    full = lambda i: (0, 0)
    vec = pl.BlockSpec((1, d), full)
    return pl.pallas_call(
        functools.partial(_outproj_kernel, alpha=alpha),
        grid=(t // tm,),
        in_specs=[pl.BlockSpec((tm, o2.shape[1]), row), pl.BlockSpec((tm, f2.shape[1]), row),
                  pl.BlockSpec((tm, d), row), vec, vec, pl.BlockSpec(wo.shape, full), vec, vec,
                  pl.BlockSpec(wq.shape, full)],
        out_specs=[pl.BlockSpec((tm, d), row), pl.BlockSpec((tm, d), row),
                   pl.BlockSpec((PEER_HEADS, tm, LANES), lambda i: (0, i, 0))],
        out_shape=[jax.ShapeDtypeStruct((t, d), F32), jax.ShapeDtypeStruct((t, d), BF16),
                   jax.ShapeDtypeStruct((PEER_HEADS, t, LANES), BF16)],
        compiler_params=_params(("parallel",)),
        name="out_proj",
    )(o2, f2, x2, gi, bi, wo, g1, b1, wq)


def _topk_cols(s, pos, k, payload=None):
    return _topk_cols_multi([s], pos, k, payload)[0]


def _topk_cols_multi(scores, pos, k, payload=None):
    big = jnp.float32(1e9)
    scores = list(scores)
    outs = [([], []) for _ in scores]
    for _ in range(k):
        for n, s in enumerate(scores):
            m = jnp.max(s, axis=0, keepdims=True)
            am = jnp.min(jnp.where(s == m, pos, big), axis=0, keepdims=True)
            hit = pos == am
            outs[n][0].append(m)
            if payload is None:
                outs[n][1].append(am)
            else:
                outs[n][1].append(jnp.max(jnp.where(hit, payload, -1.0), axis=0, keepdims=True))
            scores[n] = jnp.where(hit, -jnp.inf, s)
    return outs


def _pair_candidates(s1r, i1r, s2r, i2r):
    kk = PEER_TOPK
    tb = s1r[0].shape[1]
    r16 = lax.broadcasted_iota(jnp.int32, (kk, tb), 0).astype(F32)
    r8 = lax.broadcasted_iota(jnp.int32, (SUBLANES, tb), 0).astype(F32)
    nk = float(PEER_N_KEYS)
    ninf = -jnp.inf
    cat = lambda rows: jnp.concatenate(rows, axis=0)
    s1, i1, s2, i2 = cat(s1r), cat(i1r), cat(s2r), cat(i2r)
    s1l, i1l, s2l, i2l = (cat(r[:SUBLANES]) for r in (s1r, i1r, s2r, i2r))
    groups = []
    groups.append((s1 + s2r[0], kk * r16, i1 * nk + i2r[0], None))
    groups.append((s1r[0] + s2, r16, i1r[0] * nk + i2, r16 >= 1.0))
    groups.append((s1l + s2r[1], kk * r8 + 1.0, i1l * nk + i2r[1], r8 >= 1.0))
    groups.append((s1r[1] + s2l, kk + r8, i1r[1] * nk + i2l, r8 >= 2.0))
    for b in (2, 3, 4):
        amax = kk // (b + 1) - 1
        groups.append((s1l + s2r[b], kk * r8 + float(b), i1l * nk + i2r[b],
                       (r8 >= 2.0) & (r8 <= float(amax))))
    val = jnp.concatenate([v if ok is None else jnp.where(ok, v, ninf) for v, _, _, ok in groups], axis=0)
    pos = jnp.concatenate([p for _, p, _, _ in groups], axis=0)
    eid = jnp.concatenate([e for _, _, e, _ in groups], axis=0)
    return val, pos, eid


def _retrieve(q, sk):
    tb = q.shape[0]
    key_pos = lax.broadcasted_iota(jnp.int32, (PEER_N_KEYS, tb), 0).astype(F32)
    sc = [lax.dot_general(sk[c], q, NT_DIMS, preferred_element_type=F32)
          for c in range(2)]
    (s1, i1), (s2, i2) = _topk_cols_multi(sc, key_pos, PEER_TOPK)
    val, pos, eid = _pair_candidates(s1, i1, s2, i2)
    bestr, eselr = _topk_cols(val, pos, PEER_TOPK, payload=eid)
    best = jnp.concatenate(bestr, axis=0)
    ex = jnp.exp(best - bestr[0])
    return jnp.concatenate(eselr, axis=0), ex / jnp.sum(ex, axis=0, keepdims=True)


TOK_UNROLL = 8
U_BLOCK = 256


V_BLOCK = 8192
W_PITCH = LANES + SUBLANES


def _block_transposed(table, block):
    n, d = table.shape
    return table.astype(BF16).reshape(n // block, block, d).transpose(0, 2, 1)


ROUTE_TOK = 256
SCORE_SLOTS = 4


def _route_u_kernel(qp_ref, sk_ref, x_ref, ut_ref, e_ref, coef_ref,
                    et_ref, gt_ref, gprev_ref, blk_ref, lane_ref, act_ref, s_ref):
    i = pl.program_id(0)
    tb = x_ref.shape[0]
    nsub = tb // ROUTE_TOK
    sub = U_BLOCK // LANES
    nblk = ut_ref.shape[0]
    ngroups = nblk // SCORE_SLOTS
    units_per_group = (nsub * PEER_HEADS) // ngroups

    @pl.when(i == 0)
    def _():
        gprev_ref[...] = jnp.zeros(gprev_ref.shape, F32)
        blk_ref[...] = jnp.zeros(blk_ref.shape, jnp.int32)
        lane_ref[...] = jnp.zeros(lane_ref.shape, jnp.int32)

    act_ref[...] = jnp.zeros(act_ref.shape, F32)

    def retrieve_unit(u):
        h = u % PEER_HEADS
        sb = u // PEER_HEADS
        q = qp_ref[h, pl.ds(pl.multiple_of(sb * ROUTE_TOK, ROUTE_TOK), ROUTE_TOK), :]
        eid, gates = _retrieve(q, sk_ref[h])
        r0 = pl.multiple_of(h * PEER_TOPK, PEER_TOPK)
        et_ref[sb, pl.ds(r0, PEER_TOPK), :] = eid
        gt_ref[sb, pl.ds(r0, PEER_TOPK), :] = gates

    def score(b, slot):
        s_ref[slot] = jnp.dot(x_ref[...], ut_ref[b], preferred_element_type=F32)

    def gather(b, slot):
        lane = lane_ref[...]
        blk = blk_ref[...]
        act = act_ref[...]
        for h in range(sub):
            got = jnp.take_along_axis(s_ref[slot, :, h * LANES:(h + 1) * LANES], lane, axis=1)
            act = jnp.where(blk == b * sub + h, got, act)
        act_ref[...] = act

    def group(g, last):
        for r in range(SCORE_SLOTS):
            b = SCORE_SLOTS * g + r
            if not (last and r + 1 == SCORE_SLOTS):
                score(b + 1, (r + 1) % SCORE_SLOTS)
            gather(b, r)
            if r < units_per_group:
                retrieve_unit(units_per_group * g + r)

    score(0, 0)
    lax.fori_loop(0, ngroups - 1, lambda g, c: (group(g, False), c)[1], 0)
    group(ngroups - 1, True)
    coef_ref[...] = gprev_ref[...] * jax.nn.gelu(act_ref[...])

    for sb in range(nsub):
        rows = pl.ds(sb * ROUTE_TOK, ROUTE_TOK)
        e_nat = et_ref[sb].T.astype(jnp.int32)
        e_ref[rows, :] = e_nat
        blk_ref[rows, :] = e_nat // LANES
        lane_ref[rows, :] = e_nat % LANES
        gprev_ref[rows, :] = gt_ref[sb].T


def _route_u(qp_h, sk_ext, x1b, ut, tb=1024):
    t, d = x1b.shape
    nb = t // tb
    cur = lambda i: (jnp.minimum(i, nb - 1), 0)
    prev = lambda i: (jnp.maximum(i - 1, 0), 0)
    assert (tb // ROUTE_TOK * PEER_HEADS) % (ut.shape[0] // SCORE_SLOTS) == 0
    return pl.pallas_call(
        _route_u_kernel,
        grid=(nb + 1,),
        in_specs=[pl.BlockSpec((PEER_HEADS, tb, LANES), lambda i: (0, jnp.minimum(i, nb - 1), 0)),
                  pl.BlockSpec(memory_space=pltpu.VMEM),
                  pl.BlockSpec((tb, d), prev), pl.BlockSpec(memory_space=pltpu.VMEM)],
        out_specs=[pl.BlockSpec((tb, PICKS), cur), pl.BlockSpec((tb, PICKS), prev)],
        out_shape=[jax.ShapeDtypeStruct((t, PICKS), jnp.int32), jax.ShapeDtypeStruct((t, PICKS), F32)],
        scratch_shapes=[pltpu.VMEM((tb // ROUTE_TOK, PICKS, ROUTE_TOK), F32),
                        pltpu.VMEM((tb // ROUTE_TOK, PICKS, ROUTE_TOK), F32),
                        pltpu.VMEM((tb, PICKS), F32),
                        pltpu.VMEM((tb, PICKS), jnp.int32), pltpu.VMEM((tb, PICKS), jnp.int32),
                        pltpu.VMEM((tb, PICKS), F32), pltpu.VMEM((SCORE_SLOTS, tb, U_BLOCK), F32)],
        compiler_params=_params(("arbitrary",)),
        name="route_u",
    )(qp_h, sk_ext, x1b, ut)


def _peer_v_kernel(e_ref, coef_ref, x_ref, g_ref, b_ref, vt_ref, y_ref, w_ref, acc_ref, *, alpha):
    tb, d = x_ref.shape
    rows = lax.broadcasted_iota(jnp.int32, (LANES, PICKS), 0)

    def scatter_group(first):
        eg = e_ref[pl.ds(first, TOK_UNROLL), :]
        cg = coef_ref[pl.ds(first, TOK_UNROLL), :]
        ig = eg // LANES
        jg = eg % LANES
        for tt in range(TOK_UNROLL):
            ib = jnp.broadcast_to(ig[tt:tt + 1, :], rows.shape)
            jb = jnp.broadcast_to(jg[tt:tt + 1, :], rows.shape)
            cb = jnp.broadcast_to(cg[tt:tt + 1, :], rows.shape)
            hot_i = jnp.where(ib == rows, 1.0, 0.0).astype(BF16)
            hot_j = jnp.where(jb == rows, cb, 0.0).astype(BF16)
            wt = lax.dot_general(hot_i, hot_j, NT_DIMS, preferred_element_type=F32)
            w_ref[pl.ds(pl.multiple_of((first + tt) * W_PITCH, SUBLANES), LANES), :] = wt

    groups = 4

    def scatter(it, carry):
        for gi in range(groups):
            scatter_group(pl.multiple_of((it * groups + gi) * TOK_UNROLL, TOK_UNROLL))
        return carry

    lax.fori_loop(0, tb // (TOK_UNROLL * groups), scatter, 0)

    acc_ref[...] = jnp.zeros(acc_ref.shape, F32)
    sub = vt_ref.shape[2] // LANES

    for bp in range(vt_ref.shape[0]):
        parts = [w_ref[pl.ds(bp * sub + h, tb, stride=W_PITCH), :] for h in range(sub)]
        wp = jnp.concatenate(parts, axis=1).astype(BF16)
        acc_ref[...] += lax.dot_general(vt_ref[bp], wp, NT_DIMS, preferred_element_type=F32)
    y_ref[...] = _layer_norm(alpha * x_ref[...] + acc_ref[...].T, g_ref[...], b_ref[...])


def _peer_v(eidx, coef, x1, g2, b2, vt, alpha, tb=256):
    t, d = x1.shape
    row = lambda i: (i, 0)
    vec = pl.BlockSpec((1, d), lambda i: (0, 0))
    return pl.pallas_call(
        functools.partial(_peer_v_kernel, alpha=alpha),
        grid=(t // tb,),
        in_specs=[pl.BlockSpec((tb, PICKS), row), pl.BlockSpec((tb, PICKS), row),
                  pl.BlockSpec((tb, d), row), vec, vec, pl.BlockSpec(memory_space=pltpu.VMEM)],
        out_specs=pl.BlockSpec((tb, d), row),
        out_shape=jax.ShapeDtypeStruct((t, d), F32),
        scratch_shapes=[pltpu.VMEM((tb * W_PITCH, LANES), F32), pltpu.VMEM((d, tb), F32)],
        compiler_params=_params(("arbitrary",), vmem=BIG_VMEM_LIMIT),
        name="peer_v",
    )(eidx, coef, x1, g2, b2, vt)


def _trunk(x, p):
    b, s, d = x.shape
    t = b * s
    depth = p["w_in"].shape[0]
    alpha = (2 * depth) ** 0.25
    x2 = x.reshape(t, d)
    cur = x2
    gi, bi = p["ln_in_g"].reshape(1, d), p["ln_in_b"].reshape(1, d)
    for l in range(depth):
        assert l == 0, "single-layer trunk"
        linit = 0.8 - 0.6 * math.exp(-0.3 * l)
        aw = N_ATTN_HEADS * ATTN_HEAD_V
        w_in = p["w_in"][l]
        wab = _prep(w_in[:, 3 * aw:], p["w_fourier"][l])
        q, k, v, a, bm = _in_proj(cur, gi, bi, w_in[:, :3 * aw].astype(BF16), wab)
        lamv = jnp.stack([p["lambda_q1"][l], p["lambda_k1"][l], p["lambda_q2"][l], p["lambda_k2"][l]])
        hh = jnp.arange(1, N_ATTN_HEADS + 1, dtype=F32)
        slopes = jnp.exp2(-8.0 * hh / N_ATTN_HEADS)
        o = _attention(q.reshape(b, s, aw), k.reshape(b, s, aw), v.reshape(b, s, aw), lamv, slopes,
                       p["subln_g"][l].reshape(1, ATTN_HEAD_V), linit)
        f = _fourier(a.reshape(b, s, -1), bm.reshape(b, s, -1), p["b_fourier"][l].reshape(1, -1))
        x1, x1b, qp_h = _out_proj(o.reshape(t, aw), f.reshape(t, -1), cur, gi, bi,
                                  p["w_out"][l].astype(BF16), p["ln1_g"][l].reshape(1, d),
                                  p["ln1_b"][l].reshape(1, d), p["peer_wq"][l].astype(BF16), alpha)
        sk = p["peer_subkeys"][l]
        z = jnp.zeros_like(sk)
        sk_ext = jnp.stack([jnp.concatenate([sk[:, 0], z[:, 0]], axis=-1),
                            jnp.concatenate([z[:, 1], sk[:, 1]], axis=-1)], axis=1).astype(BF16)
        eidx, coef = _route_u(qp_h, sk_ext, x1b, _block_transposed(p["peer_u"][l], U_BLOCK))
        cur = _peer_v(eidx, coef, x1, p["ln2_g"][l].reshape(1, d), p["ln2_b"][l].reshape(1, d),
                      _block_transposed(p["peer_v"][l], V_BLOCK), alpha)
    return cur.reshape(b, s, d)


def kernel(x_prompt, x_sample, ln_in_g, ln_in_b, w_in, lambda_q1, lambda_k1, lambda_q2, lambda_k2,
           subln_g, w_fourier, b_fourier, w_out, ln1_g, ln1_b, peer_wq, peer_subkeys, peer_u, peer_v,
           ln2_g, ln2_b):
    p = dict(ln_in_g=ln_in_g, ln_in_b=ln_in_b, w_in=w_in, lambda_q1=lambda_q1, lambda_k1=lambda_k1,
             lambda_q2=lambda_q2, lambda_k2=lambda_k2, subln_g=subln_g, w_fourier=w_fourier,
             b_fourier=b_fourier, w_out=w_out, ln1_g=ln1_g, ln1_b=ln1_b, peer_wq=peer_wq,
             peer_subkeys=peer_subkeys, peer_u=peer_u, peer_v=peer_v, ln2_g=ln2_g, ln2_b=ln2_b)
    return (_trunk(x_prompt, p), _trunk(x_sample, p))
```

```python
import functools
import math

import jax
import jax.numpy as jnp
from jax import lax
from jax.experimental import pallas as pl
from jax.experimental.pallas import tpu as pltpu

F32 = jnp.float32
BF16 = jnp.bfloat16

LN_EPS = 1e-5
RMS_EPS = 1e-5
N_ATTN_HEADS = 4
ATTN_HEAD_V = 128
ATTN_HEAD_QK = 64
N_FOURIER_GROUPS = 4
FOURIER_GROUP = 128
PEER_HEADS = 8
PEER_N_KEYS = 128
PEER_TOPK = 16
PICKS = PEER_HEADS * PEER_TOPK

LANES = 128
SUBLANES = 8
VMEM_LIMIT = 56 * 1024 * 1024
BIG_VMEM_LIMIT = 60 * 1024 * 1024

NT_DIMS = (((1,), (1,)), ((), ()))
LOG2E = 1.4426950408889634


def _params(sem, vmem=VMEM_LIMIT):
    return pltpu.CompilerParams(dimension_semantics=sem, vmem_limit_bytes=vmem)


def _layer_norm(x, g, b):
    mu = jnp.mean(x, axis=-1, keepdims=True)
    xc = x - mu
    var = jnp.mean(xc * xc, axis=-1, keepdims=True)
    return xc * lax.rsqrt(var + LN_EPS) * g + b


def _prep_kernel(win_f_ref, wf_ref, c_ref, s_ref, wab_ref):
    hp = lax.Precision.HIGHEST
    for g in range(N_FOURIER_GROUPS):
        wg = wf_ref[g]
        cw = jnp.dot(c_ref[...], wg, precision=hp, preferred_element_type=F32)
        sw = jnp.dot(s_ref[...], wg, precision=hp, preferred_element_type=F32)
        wi = win_f_ref[:, g * FOURIER_GROUP:(g + 1) * FOURIER_GROUP]
        a = jnp.dot(wi, cw, precision=hp, preferred_element_type=F32)
        b = jnp.dot(wi, sw, precision=hp, preferred_element_type=F32)
        wab_ref[:, g * FOURIER_GROUP:(g + 1) * FOURIER_GROUP] = a.astype(BF16)
        wab_ref[:, (N_FOURIER_GROUPS + g) * FOURIER_GROUP:
                (N_FOURIER_GROUPS + g + 1) * FOURIER_GROUP] = b.astype(BF16)


def _prep(win_f, w_fourier):
    d = win_f.shape[0]
    n = FOURIER_GROUP
    idx = jnp.arange(n, dtype=jnp.int32)
    ang = ((idx[:, None] * idx[None, :]) % n).astype(F32) * (2.0 * math.pi / n)
    cmat = jnp.cos(ang) * (n ** -0.5)
    smat = jnp.sin(ang) * (n ** -0.5)
    return pl.pallas_call(
        _prep_kernel,
        out_shape=jax.ShapeDtypeStruct((d, 2 * N_FOURIER_GROUPS * n), BF16),
        name="prep",
    )(win_f, w_fourier, cmat, smat)


def _inproj_kernel(x_ref, g_ref, b_ref, wqkv_ref, wab_ref, q_ref, k_ref, v_ref, a_ref, bb_ref):
    xn = _layer_norm(x_ref[...], g_ref[...], b_ref[...]).astype(BF16)
    h = jnp.dot(xn, wqkv_ref[...], preferred_element_type=F32)
    aw = q_ref.shape[1]
    q_ref[...] = (h[:, :aw] * (ATTN_HEAD_QK ** -0.5)).astype(BF16)
    k_ref[...] = h[:, aw:2 * aw].astype(BF16)
    v_ref[...] = h[:, 2 * aw:3 * aw].astype(BF16)
    ab = jnp.dot(xn, wab_ref[...], preferred_element_type=F32)
    fw = a_ref.shape[1]
    a_ref[...] = ab[:, :fw].astype(BF16)
    bb_ref[...] = ab[:, fw:].astype(BF16)


def _in_proj(x2, g, b, wqkv, wab, tm=512):
    t, d = x2.shape
    aw = wqkv.shape[1] // 3
    fw = wab.shape[1] // 2
    row = lambda i: (i, 0)
    full = lambda i: (0, 0)
    outs = [jax.ShapeDtypeStruct((t, aw), BF16)] * 3 + [jax.ShapeDtypeStruct((t, fw), BF16)] * 2
    return pl.pallas_call(
        _inproj_kernel,
        grid=(t // tm,),
        in_specs=[pl.BlockSpec((tm, d), row), pl.BlockSpec((1, d), full), pl.BlockSpec((1, d), full),
                  pl.BlockSpec(wqkv.shape, full), pl.BlockSpec(wab.shape, full)],
        out_specs=[pl.BlockSpec((tm, aw), row)] * 3 + [pl.BlockSpec((tm, fw), row)] * 2,
        out_shape=outs,
        compiler_params=_params(("parallel",)),
        name="in_proj",
    )(x2, g, b, wqkv, wab)


def _attn_kernel(lamv_ref, slope_ref, q_ref, k_ref, v_ref, g_ref, o_ref,
                 ka_ref, s_ref, m_ref, l_ref, acc_ref, *, tq, tk, linit):
    h = pl.program_id(1)
    i = pl.program_id(2)
    s_len = k_ref.shape[0]
    nkv = s_len // tk
    nl = tk // LANES
    slope = slope_ref[h]
    lv = lamv_ref[...]
    lam = (jnp.exp(jnp.sum(lv[0:1] * lv[1:2], axis=-1, keepdims=True))
           - jnp.exp(jnp.sum(lv[2:3] * lv[3:4], axis=-1, keepdims=True)) + linit)
    dq = ATTN_HEAD_QK

    def aug_lanes(shape, first_pos, c):
        lane = lax.broadcasted_iota(jnp.int32, shape, 1)
        pos = first_pos + lax.broadcasted_iota(jnp.int32, shape, 0)
        hi = (pos // dq).astype(F32) * (slope * dq)
        lo = (pos % dq).astype(F32) * slope
        base = dq * (1 - c)
        keep = (lane < dq) if c == 0 else (lane >= dq)
        return lane - base, hi, lo, keep

    @pl.when(i == 0)
    def _():
        k = k_ref[...]
        for c in range(2):
            rel, hi, lo, keep = aug_lanes(k.shape, 0, c)
            aug = jnp.where(rel == 0, hi, jnp.where(rel == 1, lo,
                            jnp.where((rel == 2) | (rel == 3), 1.0, 0.0)))
            ka_ref[c] = jnp.where(keep, k, aug.astype(BF16))

    q = q_ref[...]
    q_left, q_right, q_diag = [], [], []
    for c in range(2):
        rel, hi, lo, keep = aug_lanes(q.shape, i * tq, c)
        aug = jnp.where((rel == 0) | (rel == 1), 1.0,
                        jnp.where(rel == 2, -hi, jnp.where(rel == 3, -lo, 0.0)))
        q_left.append(jnp.where(keep, q, aug.astype(BF16)))
        q_right.append(jnp.where(keep, q, (-aug).astype(BF16)))
        q_diag.append(jnp.where(keep, q, jnp.zeros_like(q)))

    m_ref[...] = jnp.full(m_ref.shape, -jnp.inf, F32)

    def scores(j, qv, bias):
        start = pl.multiple_of(j * tk, tk)
        for c in range(2):
            s = lax.dot_general(qv[c], ka_ref[c, pl.ds(start, tk), :], NT_DIMS,
                                preferred_element_type=F32)
            if bias is not None:
                s = s - bias
            s = s * LOG2E
            s_ref[c, j] = s
            mt = m_ref[c]
            for l in range(nl):
                mt = jnp.maximum(mt, s[:, l * LANES:(l + 1) * LANES])
            m_ref[c] = mt

    jd = (i * tq) // tk
    lax.fori_loop(0, jd, lambda j, cr: (scores(j, q_left, None), cr)[1], 0)
    qpos = (i * tq + lax.broadcasted_iota(jnp.int32, (tq, 1), 0)).astype(F32)
    kpos = (jd * tk + lax.broadcasted_iota(jnp.int32, (1, tk), 1)).astype(F32)
    scores(jd, q_diag, slope * jnp.abs(qpos - kpos))
    lax.fori_loop(jd + 1, nkv, lambda j, cr: (scores(j, q_right, None), cr)[1], 0)

    for c in range(2):
        m_ref[c] = jnp.broadcast_to(jnp.max(m_ref[c], axis=-1, keepdims=True), (tq, LANES))
    l_ref[...] = jnp.zeros(l_ref.shape, F32)
    acc_ref[...] = jnp.zeros(acc_ref.shape, F32)

    def accumulate(j, carry):
        vb = v_ref[pl.ds(pl.multiple_of(j * tk, tk), tk), :]
        for c in range(2):
            s = s_ref[c, j]
            mb = m_ref[c]
            ps = [jnp.exp2(s[:, l * LANES:(l + 1) * LANES] - mb) for l in range(nl)]
            lsum = l_ref[c]
            for p in ps:
                lsum = lsum + p
            l_ref[c] = lsum
            acc_ref[c] += jnp.dot(jnp.concatenate(ps, axis=1).astype(BF16), vb,
                                  preferred_element_type=F32)
        return carry

    lax.fori_loop(0, nkv, accumulate, 0)
    l0 = jnp.sum(l_ref[0], axis=-1, keepdims=True)
    l1 = jnp.sum(l_ref[1], axis=-1, keepdims=True)
    o = acc_ref[0] / l0 - lam * (acc_ref[1] / l1)
    y = o * lax.rsqrt(jnp.mean(o * o, axis=-1, keepdims=True) + RMS_EPS) * g_ref[...]
    o_ref[...] = (y * (1.0 - linit)).astype(o_ref.dtype)


ATTN_SCORE_BYTES = 32 * 1024 * 1024


def _attention(q, k, v, lamv, slopes, subln_g, linit, tk=1024):
    b, s, aw = q.shape
    hv = ATTN_HEAD_V
    tq = min(512, ATTN_SCORE_BYTES // (2 * 4 * s))
    assert tk % tq == 0 and s % tk == 0
    qspec = pl.BlockSpec((None, tq, hv), lambda bi, hi, qi: (bi, qi, hi))
    kvspec = pl.BlockSpec((None, s, hv), lambda bi, hi, qi: (bi, 0, hi))
    return pl.pallas_call(
        functools.partial(_attn_kernel, tq=tq, tk=tk, linit=linit),
        grid=(b, aw // hv, s // tq),
        in_specs=[pl.BlockSpec((4, ATTN_HEAD_QK), lambda bi, hi, qi: (0, 0)),
                  pl.BlockSpec(memory_space=pltpu.SMEM),
                  qspec, kvspec, kvspec,
                  pl.BlockSpec((1, hv), lambda bi, hi, qi: (0, 0))],
        out_specs=qspec,
        out_shape=jax.ShapeDtypeStruct((b, s, aw), BF16),
        scratch_shapes=[pltpu.VMEM((2, s, hv), BF16), pltpu.VMEM((2, s // tk, tq, tk), F32),
                        pltpu.VMEM((2, tq, LANES), F32), pltpu.VMEM((2, tq, LANES), F32),
                        pltpu.VMEM((2, tq, hv), F32)],
        compiler_params=_params(("parallel", "parallel", "arbitrary"), vmem=BIG_VMEM_LIMIT),
        name="attn",
    )(lamv, slopes, q, k, v, subln_g)


DFT_FINE = 64


def _fourier_kernel(ca_ref, nsa_ref, cr_ref, nsr_ref, a_ref, b_ref, bias_ref, f_ref, acc_ref):
    kk = pl.program_id(2)

    @pl.when(kk == 0)
    def _():
        acc_ref[...] = jnp.zeros(acc_ref.shape, F32)

    cr = cr_ref[...]
    nsr = nsr_ref[...]
    cos_rows, nsin_rows = [], []
    for a in range(ca_ref.shape[0]):
        ca = ca_ref[a:a + 1, :]
        nsa = nsa_ref[a:a + 1, :]
        cos_rows.append((ca * cr - nsa * nsr).astype(BF16))
        nsin_rows.append((nsa * cr + ca * nsr).astype(BF16))
    acc_ref[...] += (jnp.dot(jnp.concatenate(cos_rows, axis=0), a_ref[...], preferred_element_type=F32)
                     + jnp.dot(jnp.concatenate(nsin_rows, axis=0), b_ref[...],
                               preferred_element_type=F32))

    @pl.when(kk == pl.num_programs(2) - 1)
    def _():
        f_ref[...] = (acc_ref[...] + bias_ref[...]).astype(f_ref.dtype)


def _dft_tables(s):
    k = jnp.arange(s, dtype=jnp.int32)[None, :]

    def cos_nsin(rows, scale):
        ang = ((rows[:, None] * k) % s).astype(F32) * (2.0 * math.pi / s)
        return jnp.cos(ang) * scale, jnp.sin(ang) * (-scale)

    coarse = cos_nsin(jnp.arange(s // DFT_FINE, dtype=jnp.int32) * DFT_FINE, s ** -0.5)
    fine = cos_nsin(jnp.arange(DFT_FINE, dtype=jnp.int32), 1.0)
    return coarse + fine


def _fourier(a, bm, bias, tm=1024, tk=1024):
    b, s, fw = a.shape
    ca, nsa, cr, nsr = _dft_tables(s)
    cspec = pl.BlockSpec((tm // DFT_FINE, tk), lambda bi, i, kk: (i, kk))
    fspec = pl.BlockSpec((DFT_FINE, tk), lambda bi, i, kk: (0, kk))
    xspec = pl.BlockSpec((None, tk, fw), lambda bi, i, kk: (bi, kk, 0))
    return pl.pallas_call(
        _fourier_kernel,
        grid=(b, s // tm, s // tk),
        in_specs=[cspec, cspec, fspec, fspec, xspec, xspec,
                  pl.BlockSpec((1, fw), lambda bi, i, kk: (0, 0))],
        out_specs=pl.BlockSpec((None, tm, fw), lambda bi, i, kk: (bi, i, 0)),
        out_shape=jax.ShapeDtypeStruct((b, s, fw), BF16),
        scratch_shapes=[pltpu.VMEM((tm, fw), F32)],
        compiler_params=_params(("parallel", "parallel", "arbitrary")),
        name="fourier",
    )(ca, nsa, cr, nsr, a, bm, bias)


def _outproj_kernel(o_ref, f_ref, x_ref, gi_ref, bi_ref, wo_ref, g1_ref, b1_ref, wq_ref,
                    x1_ref, x1b_ref, qp_ref, *, alpha):
    aw = o_ref.shape[1]
    mix = (jnp.dot(o_ref[...], wo_ref[:aw, :], preferred_element_type=F32)
           + jnp.dot(f_ref[...], wo_ref[aw:, :], preferred_element_type=F32))
    xn = _layer_norm(x_ref[...], gi_ref[...], bi_ref[...])
    x1 = _layer_norm(alpha * xn + mix, g1_ref[...], b1_ref[...])
    x1_ref[...] = x1
    x1b = x1.astype(BF16)
    x1b_ref[...] = x1b
    qp = jnp.dot(x1b, wq_ref[...], preferred_element_type=F32).astype(BF16)
    for h in range(qp_ref.shape[0]):
        qp_ref[h] = qp[:, h * LANES:(h + 1) * LANES]


def _out_proj(o2, f2, x2, gi, bi, wo, g1, b1, wq, alpha, tm=512):
    t, d = x2.shape
    row = lambda i: (i, 0)
    full = lambda i: (0, 0)
    vec = pl.BlockSpec((1, d), full)
    return pl.pallas_call(
        functools.partial(_outproj_kernel, alpha=alpha),
        grid=(t // tm,),
        in_specs=[pl.BlockSpec((tm, o2.shape[1]), row), pl.BlockSpec((tm, f2.shape[1]), row),
                  pl.BlockSpec((tm, d), row), vec, vec, pl.BlockSpec(wo.shape, full), vec, vec,
                  pl.BlockSpec(wq.shape, full)],
        out_specs=[pl.BlockSpec((tm, d), row), pl.BlockSpec((tm, d), row),
                   pl.BlockSpec((PEER_HEADS, tm, LANES), lambda i: (0, i, 0))],
        out_shape=[jax.ShapeDtypeStruct((t, d), F32), jax.ShapeDtypeStruct((t, d), BF16),
                   jax.ShapeDtypeStruct((PEER_HEADS, t, LANES), BF16)],
        compiler_params=_params(("parallel",)),
        name="out_proj",
    )(o2, f2, x2, gi, bi, wo, g1, b1, wq)


def _topk_cols(s, pos, k, payload):
    big = jnp.float32(1e9)
    vals, sels = [], []
    for _ in range(k):
        m = jnp.max(s, axis=0, keepdims=True)
        am = jnp.min(jnp.where(s == m, pos, big), axis=0, keepdims=True)
        hit = pos == am
        vals.append(m)
        sels.append(jnp.max(jnp.where(hit, payload, -1.0), axis=0, keepdims=True))
        s = jnp.where(hit, -jnp.inf, s)
    return vals, sels


def _topk_packed(s, pos, k):
    big = jnp.float32(1e9)
    rank = lax.broadcasted_iota(jnp.int32, (k, s.shape[1]), 0)
    vals = jnp.zeros((k, s.shape[1]), F32)
    ids = jnp.zeros((k, s.shape[1]), F32)
    for r in range(k):
        m = jnp.max(s, axis=0, keepdims=True)
        am = jnp.min(jnp.where(s == m, pos, big), axis=0, keepdims=True)
        vals = jnp.where(rank == r, m, vals)
        ids = jnp.where(rank == r, am, ids)
        s = jnp.where(pos == am, -jnp.inf, s)
    return vals, ids


def _pair_candidates(s1, i1, s2, i2):
    kk = PEER_TOPK
    tb = s1.shape[1]
    r16 = lax.broadcasted_iota(jnp.int32, (kk, tb), 0).astype(F32)
    r8 = lax.broadcasted_iota(jnp.int32, (SUBLANES, tb), 0).astype(F32)
    nk = float(PEER_N_KEYS)
    ninf = -jnp.inf
    s1r, i1r, s2r, i2r = ([x[r:r + 1] for r in range(5)] for x in (s1, i1, s2, i2))
    s1l, i1l, s2l, i2l = (x[:SUBLANES] for x in (s1, i1, s2, i2))
    groups = []
    groups.append((s1 + s2r[0], kk * r16, i1 * nk + i2r[0], None))
    groups.append((s1r[0] + s2, r16, i1r[0] * nk + i2, r16 >= 1.0))
    groups.append((s1l + s2r[1], kk * r8 + 1.0, i1l * nk + i2r[1], r8 >= 1.0))
    groups.append((s1r[1] + s2l, kk + r8, i1r[1] * nk + i2l, r8 >= 2.0))
    for b in (2, 3, 4):
        amax = kk // (b + 1) - 1
        groups.append((s1l + s2r[b], kk * r8 + float(b), i1l * nk + i2r[b],
                       (r8 >= 2.0) & (r8 <= float(amax))))
    val = jnp.concatenate([v if ok is None else jnp.where(ok, v, ninf) for v, _, _, ok in groups], axis=0)
    pos = jnp.concatenate([p for _, p, _, _ in groups], axis=0)
    eid = jnp.concatenate([e for _, _, e, _ in groups], axis=0)
    return val, pos, eid


def _retrieve(q, sk):
    key_pos = lax.broadcasted_iota(jnp.int32, (PEER_N_KEYS, q.shape[0]), 0).astype(F32)
    sc = [lax.dot_general(sk[c], q, NT_DIMS, preferred_element_type=F32)
          for c in range(2)]
    (s1, i1), (s2, i2) = (_topk_packed(s, key_pos, PEER_TOPK) for s in sc)
    val, pos, eid = _pair_candidates(s1, i1, s2, i2)
    bestr, eselr = _topk_cols(val, pos, PEER_TOPK, payload=eid)
    best = jnp.concatenate(bestr, axis=0)
    ex = jnp.exp(best - bestr[0])
    return jnp.concatenate(eselr, axis=0), ex / jnp.sum(ex, axis=0, keepdims=True)


TOK_UNROLL = 8
U_BLOCK = 256


V_BLOCK = 8192
W_PITCH = LANES + SUBLANES


def _block_transposed(table, block):
    n, d = table.shape
    return table.astype(BF16).reshape(n // block, block, d).transpose(0, 2, 1)


ROUTE_TOK = 256
SCORE_SLOTS = 4


def _route_u_kernel(qp_ref, sk_ref, x_ref, ut_ref, e_ref, coef_ref,
                    et_ref, gt_ref, gprev_ref, blk_ref, lane_ref, act_ref, s_ref):
    i = pl.program_id(0)
    tb = x_ref.shape[0]
    nsub = tb // ROUTE_TOK
    sub = U_BLOCK // LANES
    nblk = ut_ref.shape[0]
    ngroups = nblk // SCORE_SLOTS
    units_per_group = (nsub * PEER_HEADS) // ngroups

    @pl.when(i == 0)
    def _():
        gprev_ref[...] = jnp.zeros(gprev_ref.shape, F32)
        blk_ref[...] = jnp.zeros(blk_ref.shape, jnp.int32)
        lane_ref[...] = jnp.zeros(lane_ref.shape, jnp.int32)

    act_ref[...] = jnp.zeros(act_ref.shape, F32)

    def retrieve_unit(u):
        h = u % PEER_HEADS
        sb = u // PEER_HEADS
        q = qp_ref[h, pl.ds(pl.multiple_of(sb * ROUTE_TOK, ROUTE_TOK), ROUTE_TOK), :]
        eid, gates = _retrieve(q, sk_ref[h])
        r0 = pl.multiple_of(h * PEER_TOPK, PEER_TOPK)
        et_ref[sb, pl.ds(r0, PEER_TOPK), :] = eid
        gt_ref[sb, pl.ds(r0, PEER_TOPK), :] = gates

    def score(b, slot):
        s_ref[slot] = jnp.dot(x_ref[...], ut_ref[b], preferred_element_type=F32)

    def gather(b, slot):
        lane = lane_ref[...]
        blk = blk_ref[...]
        act = act_ref[...]
        for h in range(sub):
            got = jnp.take_along_axis(s_ref[slot, :, h * LANES:(h + 1) * LANES], lane, axis=1)
            act = jnp.where(blk == b * sub + h, got, act)
        act_ref[...] = act

    def group(g, last):
        for r in range(SCORE_SLOTS):
            b = SCORE_SLOTS * g + r
            if not (last and r + 1 == SCORE_SLOTS):
                score(b + 1, (r + 1) % SCORE_SLOTS)
            gather(b, r)
            if r < units_per_group:
                retrieve_unit(units_per_group * g + r)

    score(0, 0)
    lax.fori_loop(0, ngroups - 1, lambda g, c: (group(g, False), c)[1], 0)
    group(ngroups - 1, True)
    coef_ref[...] = gprev_ref[...] * jax.nn.gelu(act_ref[...])

    for sb in range(nsub):
        rows = pl.ds(sb * ROUTE_TOK, ROUTE_TOK)
        e_nat = et_ref[sb].T.astype(jnp.int32)
        e_ref[rows, :] = e_nat
        blk_ref[rows, :] = e_nat // LANES
        lane_ref[rows, :] = e_nat % LANES
        gprev_ref[rows, :] = gt_ref[sb].T


def _route_u(qp_h, sk_ext, x1b, ut, tb=1024):
    t, d = x1b.shape
    nb = t // tb
    cur = lambda i: (jnp.minimum(i, nb - 1), 0)
    prev = lambda i: (jnp.maximum(i - 1, 0), 0)
    assert (tb // ROUTE_TOK * PEER_HEADS) % (ut.shape[0] // SCORE_SLOTS) == 0
    return pl.pallas_call(
        _route_u_kernel,
        grid=(nb + 1,),
        in_specs=[pl.BlockSpec((PEER_HEADS, tb, LANES), lambda i: (0, jnp.minimum(i, nb - 1), 0)),
                  pl.BlockSpec(memory_space=pltpu.VMEM),
                  pl.BlockSpec((tb, d), prev), pl.BlockSpec(memory_space=pltpu.VMEM)],
        out_specs=[pl.BlockSpec((tb, PICKS), cur), pl.BlockSpec((tb, PICKS), prev)],
        out_shape=[jax.ShapeDtypeStruct((t, PICKS), jnp.int32), jax.ShapeDtypeStruct((t, PICKS), F32)],
        scratch_shapes=[pltpu.VMEM((tb // ROUTE_TOK, PICKS, ROUTE_TOK), F32),
                        pltpu.VMEM((tb // ROUTE_TOK, PICKS, ROUTE_TOK), F32),
                        pltpu.VMEM((tb, PICKS), F32),
                        pltpu.VMEM((tb, PICKS), jnp.int32), pltpu.VMEM((tb, PICKS), jnp.int32),
                        pltpu.VMEM((tb, PICKS), F32), pltpu.VMEM((SCORE_SLOTS, tb, U_BLOCK), F32)],
        compiler_params=_params(("arbitrary",)),
        name="route_u",
    )(qp_h, sk_ext, x1b, ut)


def _peer_v_kernel(e_ref, coef_ref, x_ref, g_ref, b_ref, vt_ref, y_ref, w_ref, acc_ref, *, alpha):
    tb, d = x_ref.shape
    rows = lax.broadcasted_iota(jnp.int32, (LANES, PICKS), 0)

    def scatter_group(first):
        eg = e_ref[pl.ds(first, TOK_UNROLL), :]
        cg = coef_ref[pl.ds(first, TOK_UNROLL), :]
        ig = eg // LANES
        jg = eg % LANES
        for tt in range(TOK_UNROLL):
            ib = jnp.broadcast_to(ig[tt:tt + 1, :], rows.shape)
            jb = jnp.broadcast_to(jg[tt:tt + 1, :], rows.shape)
            cb = jnp.broadcast_to(cg[tt:tt + 1, :], rows.shape)
            hot_i = jnp.where(ib == rows, 1.0, 0.0).astype(BF16)
            hot_j = jnp.where(jb == rows, cb, 0.0).astype(BF16)
            wt = lax.dot_general(hot_i, hot_j, NT_DIMS, preferred_element_type=F32)
            w_ref[pl.ds(pl.multiple_of((first + tt) * W_PITCH, SUBLANES), LANES), :] = wt

    groups = 8

    def scatter(it, carry):
        for gi in range(groups):
            scatter_group(pl.multiple_of((it * groups + gi) * TOK_UNROLL, TOK_UNROLL))
        return carry

    lax.fori_loop(0, tb // (TOK_UNROLL * groups), scatter, 0)

    acc_ref[...] = jnp.zeros(acc_ref.shape, F32)
    sub = vt_ref.shape[2] // LANES

    for bp in range(vt_ref.shape[0]):
        parts = [w_ref[pl.ds(bp * sub + h, tb, stride=W_PITCH), :] for h in range(sub)]
        wp = jnp.concatenate(parts, axis=1).astype(BF16)
        acc_ref[...] += lax.dot_general(vt_ref[bp], wp, NT_DIMS, preferred_element_type=F32)
    y_ref[...] = _layer_norm(alpha * x_ref[...] + acc_ref[...].T, g_ref[...], b_ref[...])


def _peer_v(eidx, coef, x1, g2, b2, vt, alpha, tb=256):
    t, d = x1.shape
    row = lambda i: (i, 0)
    vec = pl.BlockSpec((1, d), lambda i: (0, 0))
    return pl.pallas_call(
        functools.partial(_peer_v_kernel, alpha=alpha),
        grid=(t // tb,),
        in_specs=[pl.BlockSpec((tb, PICKS), row), pl.BlockSpec((tb, PICKS), row),
                  pl.BlockSpec((tb, d), row), vec, vec, pl.BlockSpec(memory_space=pltpu.VMEM)],
        out_specs=pl.BlockSpec((tb, d), row),
        out_shape=jax.ShapeDtypeStruct((t, d), F32),
        scratch_shapes=[pltpu.VMEM((tb * W_PITCH, LANES), F32), pltpu.VMEM((d, tb), F32)],
        compiler_params=_params(("arbitrary",), vmem=BIG_VMEM_LIMIT),
        name="peer_v",
    )(eidx, coef, x1, g2, b2, vt)


def _trunk(x, p):
    b, s, d = x.shape
    t = b * s
    depth = p["w_in"].shape[0]
    alpha = (2 * depth) ** 0.25
    x2 = x.reshape(t, d)
    cur = x2
    gi, bi = p["ln_in_g"].reshape(1, d), p["ln_in_b"].reshape(1, d)
    for l in range(depth):
        assert l == 0, "single-layer trunk"
        linit = 0.8 - 0.6 * math.exp(-0.3 * l)
        aw = N_ATTN_HEADS * ATTN_HEAD_V
        w_in = p["w_in"][l]
        wab = _prep(w_in[:, 3 * aw:], p["w_fourier"][l])
        q, k, v, a, bm = _in_proj(cur, gi, bi, w_in[:, :3 * aw].astype(BF16), wab)
        lamv = jnp.stack([p["lambda_q1"][l], p["lambda_k1"][l], p["lambda_q2"][l], p["lambda_k2"][l]])
        hh = jnp.arange(1, N_ATTN_HEADS + 1, dtype=F32)
        slopes = jnp.exp2(-8.0 * hh / N_ATTN_HEADS)
        o = _attention(q.reshape(b, s, aw), k.reshape(b, s, aw), v.reshape(b, s, aw), lamv, slopes,
                       p["subln_g"][l].reshape(1, ATTN_HEAD_V), linit)
        f = _fourier(a.reshape(b, s, -1), bm.reshape(b, s, -1), p["b_fourier"][l].reshape(1, -1))
        x1, x1b, qp_h = _out_proj(o.reshape(t, aw), f.reshape(t, -1), cur, gi, bi,
                                  p["w_out"][l].astype(BF16), p["ln1_g"][l].reshape(1, d),
                                  p["ln1_b"][l].reshape(1, d), p["peer_wq"][l].astype(BF16), alpha)
        sk = p["peer_subkeys"][l]
        z = jnp.zeros_like(sk)
        sk_ext = jnp.stack([jnp.concatenate([sk[:, 0], z[:, 0]], axis=-1),
                            jnp.concatenate([z[:, 1], sk[:, 1]], axis=-1)], axis=1).astype(BF16)
        eidx, coef = _route_u(qp_h, sk_ext, x1b, _block_transposed(p["peer_u"][l], U_BLOCK))
        cur = _peer_v(eidx, coef, x1, p["ln2_g"][l].reshape(1, d), p["ln2_b"][l].reshape(1, d),
                      _block_transposed(p["peer_v"][l], V_BLOCK), alpha)
    return cur.reshape(b, s, d)


def kernel(x_prompt, x_sample, ln_in_g, ln_in_b, w_in, lambda_q1, lambda_k1, lambda_q2, lambda_k2,
           subln_g, w_fourier, b_fourier, w_out, ln1_g, ln1_b, peer_wq, peer_subkeys, peer_u, peer_v,
           ln2_g, ln2_b):
    p = dict(ln_in_g=ln_in_g, ln_in_b=ln_in_b, w_in=w_in, lambda_q1=lambda_q1, lambda_k1=lambda_k1,
             lambda_q2=lambda_q2, lambda_k2=lambda_k2, subln_g=subln_g, w_fourier=w_fourier,
             b_fourier=b_fourier, w_out=w_out, ln1_g=ln1_g, ln1_b=ln1_b, peer_wq=peer_wq,
             peer_subkeys=peer_subkeys, peer_u=peer_u, peer_v=peer_v, ln2_g=ln2_g, ln2_b=ln2_b)
    return (_trunk(x_prompt, p), _trunk(x_sample, p))
```

```python
import functools
import math

import jax
import jax.numpy as jnp
from jax import lax
from jax.experimental import pallas as pl
from jax.experimental.pallas import tpu as pltpu

F32 = jnp.float32
BF16 = jnp.bfloat16

LN_EPS = 1e-5
RMS_EPS = 1e-5
N_ATTN_HEADS = 4
ATTN_HEAD_V = 128
ATTN_HEAD_QK = 64
N_FOURIER_GROUPS = 4
FOURIER_GROUP = 128
PEER_HEADS = 8
PEER_N_KEYS = 128
PEER_TOPK = 16
PICKS = PEER_HEADS * PEER_TOPK

LANES = 128
SUBLANES = 8
VMEM_LIMIT = 56 * 1024 * 1024
BIG_VMEM_LIMIT = 60 * 1024 * 1024

NT_DIMS = (((1,), (1,)), ((), ()))
LOG2E = 1.4426950408889634


def _params(sem, vmem=VMEM_LIMIT):
    return pltpu.CompilerParams(dimension_semantics=sem, vmem_limit_bytes=vmem)


def _layer_norm(x, g, b):
    mu = jnp.mean(x, axis=-1, keepdims=True)
    xc = x - mu
    var = jnp.mean(xc * xc, axis=-1, keepdims=True)
    return xc * lax.rsqrt(var + LN_EPS) * g + b


def _prep_kernel(win_f_ref, wf_ref, c_ref, s_ref, wab_ref):
    hp = lax.Precision.HIGHEST
    for g in range(N_FOURIER_GROUPS):
        wg = wf_ref[g]
        cw = jnp.dot(c_ref[...], wg, precision=hp, preferred_element_type=F32)
        sw = jnp.dot(s_ref[...], wg, precision=hp, preferred_element_type=F32)
        wi = win_f_ref[:, g * FOURIER_GROUP:(g + 1) * FOURIER_GROUP]
        a = jnp.dot(wi, cw, precision=hp, preferred_element_type=F32)
        b = jnp.dot(wi, sw, precision=hp, preferred_element_type=F32)
        wab_ref[:, g * FOURIER_GROUP:(g + 1) * FOURIER_GROUP] = a.astype(BF16)
        wab_ref[:, (N_FOURIER_GROUPS + g) * FOURIER_GROUP:
                (N_FOURIER_GROUPS + g + 1) * FOURIER_GROUP] = b.astype(BF16)


def _prep(win_f, w_fourier):
    d = win_f.shape[0]
    n = FOURIER_GROUP
    idx = jnp.arange(n, dtype=jnp.int32)
    ang = ((idx[:, None] * idx[None, :]) % n).astype(F32) * (2.0 * math.pi / n)
    cmat = jnp.cos(ang) * (n ** -0.5)
    smat = jnp.sin(ang) * (n ** -0.5)
    return pl.pallas_call(
        _prep_kernel,
        out_shape=jax.ShapeDtypeStruct((d, 2 * N_FOURIER_GROUPS * n), BF16),
        name="prep",
    )(win_f, w_fourier, cmat, smat)


def _inproj_kernel(x_ref, g_ref, b_ref, wqkv_ref, wab_ref, q_ref, k_ref, v_ref, a_ref, bb_ref):
    xn = _layer_norm(x_ref[...], g_ref[...], b_ref[...]).astype(BF16)
    h = jnp.dot(xn, wqkv_ref[...], preferred_element_type=F32)
    aw = q_ref.shape[1]
    q_ref[...] = (h[:, :aw] * (ATTN_HEAD_QK ** -0.5)).astype(BF16)
    k_ref[...] = h[:, aw:2 * aw].astype(BF16)
    v_ref[...] = h[:, 2 * aw:3 * aw].astype(BF16)
    ab = jnp.dot(xn, wab_ref[...], preferred_element_type=F32)
    fw = a_ref.shape[1]
    a_ref[...] = ab[:, :fw].astype(BF16)
    bb_ref[...] = ab[:, fw:].astype(BF16)


def _in_proj(x2, g, b, wqkv, wab, tm=512):
    t, d = x2.shape
    aw = wqkv.shape[1] // 3
    fw = wab.shape[1] // 2
    row = lambda i: (i, 0)
    full = lambda i: (0, 0)
    outs = [jax.ShapeDtypeStruct((t, aw), BF16)] * 3 + [jax.ShapeDtypeStruct((t, fw), BF16)] * 2
    return pl.pallas_call(
        _inproj_kernel,
        grid=(t // tm,),
        in_specs=[pl.BlockSpec((tm, d), row), pl.BlockSpec((1, d), full), pl.BlockSpec((1, d), full),
                  pl.BlockSpec(wqkv.shape, full), pl.BlockSpec(wab.shape, full)],
        out_specs=[pl.BlockSpec((tm, aw), row)] * 3 + [pl.BlockSpec((tm, fw), row)] * 2,
        out_shape=outs,
        compiler_params=_params(("parallel",)),
        name="in_proj",
    )(x2, g, b, wqkv, wab)


def _attn_kernel(lamv_ref, slope_ref, q_ref, k_ref, v_ref, g_ref, o_ref,
                 ka_ref, s_ref, m_ref, l_ref, acc_ref, *, tq, tk, linit):
    h = pl.program_id(1)
    i = pl.program_id(2)
    s_len = k_ref.shape[0]
    nkv = s_len // tk
    nl = tk // LANES
    slope = slope_ref[h]
    lv = lamv_ref[...]
    lam = (jnp.exp(jnp.sum(lv[0:1] * lv[1:2], axis=-1, keepdims=True))
           - jnp.exp(jnp.sum(lv[2:3] * lv[3:4], axis=-1, keepdims=True)) + linit)
    dq = ATTN_HEAD_QK

    def aug_lanes(shape, first_pos, c):
        lane = lax.broadcasted_iota(jnp.int32, shape, 1)
        pos = first_pos + lax.broadcasted_iota(jnp.int32, shape, 0)
        hi = (pos // dq).astype(F32) * (slope * dq)
        lo = (pos % dq).astype(F32) * slope
        base = dq * (1 - c)
        keep = (lane < dq) if c == 0 else (lane >= dq)
        return lane - base, hi, lo, keep

    @pl.when(i == 0)
    def _():
        k = k_ref[...]
        for c in range(2):
            rel, hi, lo, keep = aug_lanes(k.shape, 0, c)
            aug = jnp.where(rel == 0, hi, jnp.where(rel == 1, lo,
                            jnp.where((rel == 2) | (rel == 3), 1.0, 0.0)))
            ka_ref[c] = jnp.where(keep, k, aug.astype(BF16))

    q = q_ref[...]
    q_left, q_right, q_diag = [], [], []
    for c in range(2):
        rel, hi, lo, keep = aug_lanes(q.shape, i * tq, c)
        aug = jnp.where((rel == 0) | (rel == 1), 1.0,
                        jnp.where(rel == 2, -hi, jnp.where(rel == 3, -lo, 0.0)))
        q_left.append(jnp.where(keep, q, aug.astype(BF16)))
        q_right.append(jnp.where(keep, q, (-aug).astype(BF16)))
        q_diag.append(jnp.where(keep, q, jnp.zeros_like(q)))

    m_ref[...] = jnp.full(m_ref.shape, -jnp.inf, F32)

    def scores(j, qv, bias):
        start = pl.multiple_of(j * tk, tk)
        for c in range(2):
            s = lax.dot_general(qv[c], ka_ref[c, pl.ds(start, tk), :], NT_DIMS,
                                preferred_element_type=F32)
            if bias is not None:
                s = s - bias
            s = s * LOG2E
            s_ref[c, j] = s
            mt = m_ref[c]
            for l in range(nl):
                mt = jnp.maximum(mt, s[:, l * LANES:(l + 1) * LANES])
            m_ref[c] = mt

    jd = (i * tq) // tk
    lax.fori_loop(0, jd, lambda j, cr: (scores(j, q_left, None), cr)[1], 0)
    qpos = (i * tq + lax.broadcasted_iota(jnp.int32, (tq, 1), 0)).astype(F32)
    kpos = (jd * tk + lax.broadcasted_iota(jnp.int32, (1, tk), 1)).astype(F32)
    scores(jd, q_diag, slope * jnp.abs(qpos - kpos))
    lax.fori_loop(jd + 1, nkv, lambda j, cr: (scores(j, q_right, None), cr)[1], 0)

    for c in range(2):
        m_ref[c] = jnp.broadcast_to(jnp.max(m_ref[c], axis=-1, keepdims=True), (tq, LANES))
    l_ref[...] = jnp.zeros(l_ref.shape, F32)
    acc_ref[...] = jnp.zeros(acc_ref.shape, F32)

    def accumulate(j, carry):
        vb = v_ref[pl.ds(pl.multiple_of(j * tk, tk), tk), :]
        for c in range(2):
            s = s_ref[c, j]
            mb = m_ref[c]
            ps = [jnp.exp2(s[:, l * LANES:(l + 1) * LANES] - mb) for l in range(nl)]
            lsum = l_ref[c]
            for p in ps:
                lsum = lsum + p
            l_ref[c] = lsum
            acc_ref[c] += jnp.dot(jnp.concatenate(ps, axis=1).astype(BF16), vb,
                                  preferred_element_type=F32)
        return carry

    lax.fori_loop(0, nkv, accumulate, 0)
    l0 = jnp.sum(l_ref[0], axis=-1, keepdims=True)
    l1 = jnp.sum(l_ref[1], axis=-1, keepdims=True)
    o = acc_ref[0] / l0 - lam * (acc_ref[1] / l1)
    y = o * lax.rsqrt(jnp.mean(o * o, axis=-1, keepdims=True) + RMS_EPS) * g_ref[...]
    o_ref[...] = (y * (1.0 - linit)).astype(o_ref.dtype)


ATTN_SCORE_BYTES = 32 * 1024 * 1024


def _attention(q, k, v, lamv, slopes, subln_g, linit, tk=1024):
    b, s, aw = q.shape
    hv = ATTN_HEAD_V
    tq = min(tk, ATTN_SCORE_BYTES // (2 * 4 * s))
    assert tk % tq == 0 and s % tk == 0
    qspec = pl.BlockSpec((None, tq, hv), lambda bi, hi, qi: (bi, qi, hi))
    kvspec = pl.BlockSpec((None, s, hv), lambda bi, hi, qi: (bi, 0, hi))
    return pl.pallas_call(
        functools.partial(_attn_kernel, tq=tq, tk=tk, linit=linit),
        grid=(b, aw // hv, s // tq),
        in_specs=[pl.BlockSpec((4, ATTN_HEAD_QK), lambda bi, hi, qi: (0, 0)),
                  pl.BlockSpec(memory_space=pltpu.SMEM),
                  qspec, kvspec, kvspec,
                  pl.BlockSpec((1, hv), lambda bi, hi, qi: (0, 0))],
        out_specs=qspec,
        out_shape=jax.ShapeDtypeStruct((b, s, aw), BF16),
        scratch_shapes=[pltpu.VMEM((2, s, hv), BF16), pltpu.VMEM((2, s // tk, tq, tk), F32),
                        pltpu.VMEM((2, tq, LANES), F32), pltpu.VMEM((2, tq, LANES), F32),
                        pltpu.VMEM((2, tq, hv), F32)],
        compiler_params=_params(("parallel", "parallel", "arbitrary"), vmem=BIG_VMEM_LIMIT),
        name="attn",
    )(lamv, slopes, q, k, v, subln_g)


DFT_FINE = 64


def _fourier_kernel(ca_ref, nsa_ref, cr_ref, nsr_ref, a_ref, b_ref, bias_ref, f_ref, acc_ref):
    kk = pl.program_id(2)

    @pl.when(kk == 0)
    def _():
        acc_ref[...] = jnp.zeros(acc_ref.shape, F32)

    cr = cr_ref[...]
    nsr = nsr_ref[...]
    cos_rows, nsin_rows = [], []
    for a in range(ca_ref.shape[0]):
        ca = ca_ref[a:a + 1, :]
        nsa = nsa_ref[a:a + 1, :]
        cos_rows.append((ca * cr - nsa * nsr).astype(BF16))
        nsin_rows.append((nsa * cr + ca * nsr).astype(BF16))
    acc_ref[...] += (jnp.dot(jnp.concatenate(cos_rows, axis=0), a_ref[...], preferred_element_type=F32)
                     + jnp.dot(jnp.concatenate(nsin_rows, axis=0), b_ref[...],
                               preferred_element_type=F32))

    @pl.when(kk == pl.num_programs(2) - 1)
    def _():
        f_ref[...] = (acc_ref[...] + bias_ref[...]).astype(f_ref.dtype)


def _dft_tables(s):
    k = jnp.arange(s, dtype=jnp.int32)[None, :]

    def cos_nsin(rows, scale):
        ang = ((rows[:, None] * k) % s).astype(F32) * (2.0 * math.pi / s)
        return jnp.cos(ang) * scale, jnp.sin(ang) * (-scale)

    coarse = cos_nsin(jnp.arange(s // DFT_FINE, dtype=jnp.int32) * DFT_FINE, s ** -0.5)
    fine = cos_nsin(jnp.arange(DFT_FINE, dtype=jnp.int32), 1.0)
    return coarse + fine


def _fourier(a, bm, bias, tm=1024, tk=1024):
    b, s, fw = a.shape
    ca, nsa, cr, nsr = _dft_tables(s)
    cspec = pl.BlockSpec((tm // DFT_FINE, tk), lambda bi, i, kk: (i, kk))
    fspec = pl.BlockSpec((DFT_FINE, tk), lambda bi, i, kk: (0, kk))
    xspec = pl.BlockSpec((None, tk, fw), lambda bi, i, kk: (bi, kk, 0))
    return pl.pallas_call(
        _fourier_kernel,
        grid=(b, s // tm, s // tk),
        in_specs=[cspec, cspec, fspec, fspec, xspec, xspec,
                  pl.BlockSpec((1, fw), lambda bi, i, kk: (0, 0))],
        out_specs=pl.BlockSpec((None, tm, fw), lambda bi, i, kk: (bi, i, 0)),
        out_shape=jax.ShapeDtypeStruct((b, s, fw), BF16),
        scratch_shapes=[pltpu.VMEM((tm, fw), F32)],
        compiler_params=_params(("parallel", "parallel", "arbitrary")),
        name="fourier",
    )(ca, nsa, cr, nsr, a, bm, bias)


def _outproj_kernel(o_ref, f_ref, x_ref, gi_ref, bi_ref, wo_ref, g1_ref, b1_ref, wq_ref,
                    x1_ref, x1b_ref, qp_ref, *, alpha):
    aw = o_ref.shape[1]
    mix = (jnp.dot(o_ref[...], wo_ref[:aw, :], preferred_element_type=F32)
           + jnp.dot(f_ref[...], wo_ref[aw:, :], preferred_element_type=F32))
    xn = _layer_norm(x_ref[...], gi_ref[...], bi_ref[...])
    x1 = _layer_norm(alpha * xn + mix, g1_ref[...], b1_ref[...])
    x1_ref[...] = x1
    x1b = x1.astype(BF16)
    x1b_ref[...] = x1b
    qp = jnp.dot(x1b, wq_ref[...], preferred_element_type=F32).astype(BF16)
    for h in range(qp_ref.shape[0]):
        qp_ref[h] = qp[:, h * LANES:(h + 1) * LANES]


def _out_proj(o2, f2, x2, gi, bi, wo, g1, b1, wq, alpha, tm=512):
    t, d = x2.shape
    row = lambda i: (i, 0)
    full = lambda i: (0, 0)
    vec = pl.BlockSpec((1, d), full)
    return pl.pallas_call(
        functools.partial(_outproj_kernel, alpha=alpha),
        grid=(t // tm,),
        in_specs=[pl.BlockSpec((tm, o2.shape[1]), row), pl.BlockSpec((tm, f2.shape[1]), row),
                  pl.BlockSpec((tm, d), row), vec, vec, pl.BlockSpec(wo.shape, full), vec, vec,
                  pl.BlockSpec(wq.shape, full)],
        out_specs=[pl.BlockSpec((tm, d), row), pl.BlockSpec((tm, d), row),
                   pl.BlockSpec((PEER_HEADS, tm, LANES), lambda i: (0, i, 0))],
        out_shape=[jax.ShapeDtypeStruct((t, d), F32), jax.ShapeDtypeStruct((t, d), BF16),
                   jax.ShapeDtypeStruct((PEER_HEADS, t, LANES), BF16)],
        compiler_params=_params(("parallel",)),
        name="out_proj",
    )(o2, f2, x2, gi, bi, wo, g1, b1, wq)


def _topk_cols(s, pos, k, payload):
    big = jnp.float32(1e9)
    vals, sels = [], []
    for _ in range(k):
        m = jnp.max(s, axis=0, keepdims=True)
        am = jnp.min(jnp.where(s == m, pos, big), axis=0, keepdims=True)
        hit = pos == am
        vals.append(m)
        sels.append(jnp.max(jnp.where(hit, payload, -1.0), axis=0, keepdims=True))
        s = jnp.where(hit, -jnp.inf, s)
    return vals, sels


def _topk_packed(s, pos, k):
    big = jnp.float32(1e9)
    rank = lax.broadcasted_iota(jnp.int32, (k, s.shape[1]), 0)
    vals = jnp.zeros((k, s.shape[1]), F32)
    ids = jnp.zeros((k, s.shape[1]), F32)
    for r in range(k):
        m = jnp.max(s, axis=0, keepdims=True)
        am = jnp.min(jnp.where(s == m, pos, big), axis=0, keepdims=True)
        vals = jnp.where(rank == r, m, vals)
        ids = jnp.where(rank == r, am, ids)
        s = jnp.where(pos == am, -jnp.inf, s)
    return vals, ids


def _pair_candidates(s1, i1, s2, i2):
    kk = PEER_TOPK
    tb = s1.shape[1]
    r16 = lax.broadcasted_iota(jnp.int32, (kk, tb), 0).astype(F32)
    r8 = lax.broadcasted_iota(jnp.int32, (SUBLANES, tb), 0).astype(F32)
    nk = float(PEER_N_KEYS)
    ninf = -jnp.inf
    s1r, i1r, s2r, i2r = ([x[r:r + 1] for r in range(5)] for x in (s1, i1, s2, i2))
    s1l, i1l, s2l, i2l = (x[:SUBLANES] for x in (s1, i1, s2, i2))
    groups = []
    groups.append((s1 + s2r[0], kk * r16, i1 * nk + i2r[0], None))
    groups.append((s1r[0] + s2, r16, i1r[0] * nk + i2, r16 >= 1.0))
    groups.append((s1l + s2r[1], kk * r8 + 1.0, i1l * nk + i2r[1], r8 >= 1.0))
    groups.append((s1r[1] + s2l, kk + r8, i1r[1] * nk + i2l, r8 >= 2.0))
    for b in (2, 3, 4):
        amax = kk // (b + 1) - 1
        groups.append((s1l + s2r[b], kk * r8 + float(b), i1l * nk + i2r[b],
                       (r8 >= 2.0) & (r8 <= float(amax))))
    val = jnp.concatenate([v if ok is None else jnp.where(ok, v, ninf) for v, _, _, ok in groups], axis=0)
    pos = jnp.concatenate([p for _, p, _, _ in groups], axis=0)
    eid = jnp.concatenate([e for _, _, e, _ in groups], axis=0)
    return val, pos, eid


def _retrieve(q, sk):
    key_pos = lax.broadcasted_iota(jnp.int32, (PEER_N_KEYS, q.shape[0]), 0).astype(F32)
    sc = [lax.dot_general(sk[c], q, NT_DIMS, preferred_element_type=F32)
          for c in range(2)]
    (s1, i1), (s2, i2) = (_topk_packed(s, key_pos, PEER_TOPK) for s in sc)
    val, pos, eid = _pair_candidates(s1, i1, s2, i2)
    bestr, eselr = _topk_cols(val, pos, PEER_TOPK, payload=eid)
    best = jnp.concatenate(bestr, axis=0)
    ex = jnp.exp(best - bestr[0])
    return jnp.concatenate(eselr, axis=0), ex / jnp.sum(ex, axis=0, keepdims=True)


TOK_UNROLL = 8
U_BLOCK = 256


V_BLOCK = 8192
W_PITCH = LANES + SUBLANES


def _block_transposed(table, block):
    n, d = table.shape
    return table.astype(BF16).reshape(n // block, block, d).transpose(0, 2, 1)


ROUTE_TOK = 256
SCORE_SLOTS = 4


def _route_u_kernel(qp_ref, sk_ref, x_ref, ut_ref, e_ref, coef_ref,
                    et_ref, gt_ref, gprev_ref, blk_ref, lane_ref, act_ref, s_ref):
    i = pl.program_id(0)
    tb = x_ref.shape[0]
    nsub = tb // ROUTE_TOK
    sub = U_BLOCK // LANES
    nblk = ut_ref.shape[0]
    ngroups = nblk // SCORE_SLOTS
    units_per_group = (nsub * PEER_HEADS) // ngroups

    @pl.when(i == 0)
    def _():
        gprev_ref[...] = jnp.zeros(gprev_ref.shape, F32)
        blk_ref[...] = jnp.zeros(blk_ref.shape, jnp.int32)
        lane_ref[...] = jnp.zeros(lane_ref.shape, jnp.int32)

    act_ref[...] = jnp.zeros(act_ref.shape, F32)

    def retrieve_unit(u):
        h = u % PEER_HEADS
        sb = u // PEER_HEADS
        q = qp_ref[h, pl.ds(pl.multiple_of(sb * ROUTE_TOK, ROUTE_TOK), ROUTE_TOK), :]
        eid, gates = _retrieve(q, sk_ref[h])
        r0 = pl.multiple_of(h * PEER_TOPK, PEER_TOPK)
        et_ref[sb, pl.ds(r0, PEER_TOPK), :] = eid
        gt_ref[sb, pl.ds(r0, PEER_TOPK), :] = gates

    def score(b, slot):
        s_ref[slot] = jnp.dot(x_ref[...], ut_ref[b], preferred_element_type=F32)

    def gather(b, slot):
        lane = lane_ref[...]
        blk = blk_ref[...]
        act = act_ref[...]
        for h in range(sub):
            got = jnp.take_along_axis(s_ref[slot, :, h * LANES:(h + 1) * LANES], lane, axis=1)
            act = jnp.where(blk == b * sub + h, got, act)
        act_ref[...] = act

    def group(g, last):
        for r in range(SCORE_SLOTS):
            b = SCORE_SLOTS * g + r
            if not (last and r + 1 == SCORE_SLOTS):
                score(b + 1, (r + 1) % SCORE_SLOTS)
            gather(b, r)
            if r < units_per_group:
                retrieve_unit(units_per_group * g + r)

    score(0, 0)
    lax.fori_loop(0, ngroups - 1, lambda g, c: (group(g, False), c)[1], 0)
    group(ngroups - 1, True)
    coef_ref[...] = gprev_ref[...] * jax.nn.gelu(act_ref[...])

    for sb in range(nsub):
        rows = pl.ds(sb * ROUTE_TOK, ROUTE_TOK)
        e_nat = et_ref[sb].T.astype(jnp.int32)
        e_ref[rows, :] = e_nat
        blk_ref[rows, :] = e_nat // LANES
        lane_ref[rows, :] = e_nat % LANES
        gprev_ref[rows, :] = gt_ref[sb].T


def _route_u(qp_h, sk_ext, x1b, ut, tb=1024):
    t, d = x1b.shape
    nb = t // tb
    cur = lambda i: (jnp.minimum(i, nb - 1), 0)
    prev = lambda i: (jnp.maximum(i - 1, 0), 0)
    assert (tb // ROUTE_TOK * PEER_HEADS) % (ut.shape[0] // SCORE_SLOTS) == 0
    return pl.pallas_call(
        _route_u_kernel,
        grid=(nb + 1,),
        in_specs=[pl.BlockSpec((PEER_HEADS, tb, LANES), lambda i: (0, jnp.minimum(i, nb - 1), 0)),
                  pl.BlockSpec(memory_space=pltpu.VMEM),
                  pl.BlockSpec((tb, d), prev), pl.BlockSpec(memory_space=pltpu.VMEM)],
        out_specs=[pl.BlockSpec((tb, PICKS), cur), pl.BlockSpec((tb, PICKS), prev)],
        out_shape=[jax.ShapeDtypeStruct((t, PICKS), jnp.int32), jax.ShapeDtypeStruct((t, PICKS), F32)],
        scratch_shapes=[pltpu.VMEM((tb // ROUTE_TOK, PICKS, ROUTE_TOK), F32),
                        pltpu.VMEM((tb // ROUTE_TOK, PICKS, ROUTE_TOK), F32),
                        pltpu.VMEM((tb, PICKS), F32),
                        pltpu.VMEM((tb, PICKS), jnp.int32), pltpu.VMEM((tb, PICKS), jnp.int32),
                        pltpu.VMEM((tb, PICKS), F32), pltpu.VMEM((SCORE_SLOTS, tb, U_BLOCK), F32)],
        compiler_params=_params(("arbitrary",)),
        name="route_u",
    )(qp_h, sk_ext, x1b, ut)


def _peer_v_kernel(e_ref, coef_ref, x_ref, g_ref, b_ref, vt_ref, y_ref, w_ref, acc_ref, *, alpha):
    tb, d = x_ref.shape
    rows = lax.broadcasted_iota(jnp.int32, (LANES, PICKS), 0)

    def scatter_group(first):
        eg = e_ref[pl.ds(first, TOK_UNROLL), :]
        cg = coef_ref[pl.ds(first, TOK_UNROLL), :]
        ig = eg // LANES
        jg = eg % LANES
        for tt in range(TOK_UNROLL):
            ib = jnp.broadcast_to(ig[tt:tt + 1, :], rows.shape)
            jb = jnp.broadcast_to(jg[tt:tt + 1, :], rows.shape)
            cb = jnp.broadcast_to(cg[tt:tt + 1, :], rows.shape)
            hot_i = jnp.where(ib == rows, 1.0, 0.0).astype(BF16)
            hot_j = jnp.where(jb == rows, cb, 0.0).astype(BF16)
            wt = lax.dot_general(hot_i, hot_j, NT_DIMS, preferred_element_type=F32)
            w_ref[pl.ds(pl.multiple_of((first + tt) * W_PITCH, SUBLANES), LANES), :] = wt

    groups = 16

    def scatter(it, carry):
        for gi in range(groups):
            scatter_group(pl.multiple_of((it * groups + gi) * TOK_UNROLL, TOK_UNROLL))
        return carry

    lax.fori_loop(0, tb // (TOK_UNROLL * groups), scatter, 0)

    acc_ref[...] = jnp.zeros(acc_ref.shape, F32)
    sub = vt_ref.shape[2] // LANES

    for bp in range(vt_ref.shape[0]):
        parts = [w_ref[pl.ds(bp * sub + h, tb, stride=W_PITCH), :] for h in range(sub)]
        wp = jnp.concatenate(parts, axis=1).astype(BF16)
        acc_ref[...] += lax.dot_general(vt_ref[bp], wp, NT_DIMS, preferred_element_type=F32)
    y_ref[...] = _layer_norm(alpha * x_ref[...] + acc_ref[...].T, g_ref[...], b_ref[...])


def _peer_v(eidx, coef, x1, g2, b2, vt, alpha, tb=256):
    t, d = x1.shape
    row = lambda i: (i, 0)
    vec = pl.BlockSpec((1, d), lambda i: (0, 0))
    return pl.pallas_call(
        functools.partial(_peer_v_kernel, alpha=alpha),
        grid=(t // tb,),
        in_specs=[pl.BlockSpec((tb, PICKS), row), pl.BlockSpec((tb, PICKS), row),
                  pl.BlockSpec((tb, d), row), vec, vec, pl.BlockSpec(memory_space=pltpu.VMEM)],
        out_specs=pl.BlockSpec((tb, d), row),
        out_shape=jax.ShapeDtypeStruct((t, d), F32),
        scratch_shapes=[pltpu.VMEM((tb * W_PITCH, LANES), F32), pltpu.VMEM((d, tb), F32)],
        compiler_params=_params(("arbitrary",), vmem=BIG_VMEM_LIMIT),
        name="peer_v",
    )(eidx, coef, x1, g2, b2, vt)


def _trunk(x, p):
    b, s, d = x.shape
    t = b * s
    depth = p["w_in"].shape[0]
    alpha = (2 * depth) ** 0.25
    x2 = x.reshape(t, d)
    cur = x2
    gi, bi = p["ln_in_g"].reshape(1, d), p["ln_in_b"].reshape(1, d)
    for l in range(depth):
        assert l == 0, "single-layer trunk"
        linit = 0.8 - 0.6 * math.exp(-0.3 * l)
        aw = N_ATTN_HEADS * ATTN_HEAD_V
        w_in = p["w_in"][l]
        wab = _prep(w_in[:, 3 * aw:], p["w_fourier"][l])
        q, k, v, a, bm = _in_proj(cur, gi, bi, w_in[:, :3 * aw].astype(BF16), wab)
        lamv = jnp.stack([p["lambda_q1"][l], p["lambda_k1"][l], p["lambda_q2"][l], p["lambda_k2"][l]])
        hh = jnp.arange(1, N_ATTN_HEADS + 1, dtype=F32)
        slopes = jnp.exp2(-8.0 * hh / N_ATTN_HEADS)
        o = _attention(q.reshape(b, s, aw), k.reshape(b, s, aw), v.reshape(b, s, aw), lamv, slopes,
                       p["subln_g"][l].reshape(1, ATTN_HEAD_V), linit)
        f = _fourier(a.reshape(b, s, -1), bm.reshape(b, s, -1), p["b_fourier"][l].reshape(1, -1))
        x1, x1b, qp_h = _out_proj(o.reshape(t, aw), f.reshape(t, -1), cur, gi, bi,
                                  p["w_out"][l].astype(BF16), p["ln1_g"][l].reshape(1, d),
                                  p["ln1_b"][l].reshape(1, d), p["peer_wq"][l].astype(BF16), alpha)
        sk = p["peer_subkeys"][l]
        z = jnp.zeros_like(sk)
        sk_ext = jnp.stack([jnp.concatenate([sk[:, 0], z[:, 0]], axis=-1),
                            jnp.concatenate([z[:, 1], sk[:, 1]], axis=-1)], axis=1).astype(BF16)
        eidx, coef = _route_u(qp_h, sk_ext, x1b, _block_transposed(p["peer_u"][l], U_BLOCK))
        cur = _peer_v(eidx, coef, x1, p["ln2_g"][l].reshape(1, d), p["ln2_b"][l].reshape(1, d),
                      _block_transposed(p["peer_v"][l], V_BLOCK), alpha)
    return cur.reshape(b, s, d)


def kernel(x_prompt, x_sample, ln_in_g, ln_in_b, w_in, lambda_q1, lambda_k1, lambda_q2, lambda_k2,
           subln_g, w_fourier, b_fourier, w_out, ln1_g, ln1_b, peer_wq, peer_subkeys, peer_u, peer_v,
           ln2_g, ln2_b):
    p = dict(ln_in_g=ln_in_g, ln_in_b=ln_in_b, w_in=w_in, lambda_q1=lambda_q1, lambda_k1=lambda_k1,
             lambda_q2=lambda_q2, lambda_k2=lambda_k2, subln_g=subln_g, w_fourier=w_fourier,
             b_fourier=b_fourier, w_out=w_out, ln1_g=ln1_g, ln1_b=ln1_b, peer_wq=peer_wq,
             peer_subkeys=peer_subkeys, peer_u=peer_u, peer_v=peer_v, ln2_g=ln2_g, ln2_b=ln2_b)
    return (_trunk(x_prompt, p), _trunk(x_sample, p))
```
